```python
import jax, jax.numpy as jnp
from jax import lax
import numpy as np

D_MODEL = 1024
BATCH = 16
SEQ = 4096
DEPTH = 2

MLA_HEADS = 8
MLA_NOPE = 64
MLA_ROPE = 32
MLA_V = 64
MLA_Q_RANK = 256
MLA_KV_RANK = 128
Q_BLOCK = 128
CONV_CH = 512
CONV_WIDTH = 31
RET_HEADS = 4
RET_DK = 128
RET_DV = 256
RET_CHUNK = 128
FFN_HIDDEN = -(-8 * D_MODEL // (3 * 256)) * 256
N_BRANCH = 3
ROPE_BASE = 10000.0
EPS = 1e-6

IN_SPLITS = (MLA_Q_RANK, MLA_KV_RANK, MLA_ROPE, 2 * CONV_CH,
             RET_HEADS * RET_DK, RET_HEADS * RET_DK, RET_HEADS * RET_DV, RET_HEADS * RET_DV,
             N_BRANCH * D_MODEL)
IN_COLS = (MLA_Q_RANK + MLA_KV_RANK + MLA_ROPE + 2 * CONV_CH + 2 * RET_HEADS * RET_DK
           + 2 * RET_HEADS * RET_DV + N_BRANCH * D_MODEL)

kernel_name = "hybrid_mla_conformer_retention_encoder"


def rms_norm(x, g):
    xf = x.astype(jnp.float32)
    y = xf * lax.rsqrt(jnp.mean(xf * xf, -1, keepdims=True) + EPS)
    return (y * g.astype(jnp.float32)).astype(x.dtype)


def layer_norm(x, g, b):
    xf = x.astype(jnp.float32)
    mu = jnp.mean(xf, -1, keepdims=True)
    var = jnp.mean(jnp.square(xf - mu), -1, keepdims=True)
    y = (xf - mu) * lax.rsqrt(var + EPS)
    return (y * g.astype(jnp.float32) + b.astype(jnp.float32)).astype(x.dtype)


def rope_tables(positions, dim):
    inv = ROPE_BASE ** (-jnp.arange(0, dim, 2, dtype=jnp.float32) / dim)
    ang = positions.astype(jnp.float32)[..., None] * inv
    return jnp.cos(ang), jnp.sin(ang)


def apply_rope(x, cos, sin):
    if x.ndim == 4:
        cos, sin = cos[:, :, None, :], sin[:, :, None, :]
    cos, sin = cos.astype(x.dtype), sin.astype(x.dtype)
    x1, x2 = jnp.split(x, 2, axis=-1)
    return jnp.concatenate([x1 * cos - x2 * sin, x2 * cos + x1 * sin], axis=-1)


def mla_branch(c_q, c_kv, k_pe, cos, sin, q_norm, w_uq, kv_norm, w_ukv, w_o):
    B, S, _ = c_q.shape
    q = (rms_norm(c_q, q_norm) @ w_uq).reshape(B, S, MLA_HEADS, MLA_NOPE + MLA_ROPE)
    q_nope = q[..., :MLA_NOPE]
    q_pe = apply_rope(q[..., MLA_NOPE:], cos, sin)
    kv = (rms_norm(c_kv, kv_norm) @ w_ukv).reshape(B, S, MLA_HEADS, MLA_NOPE + MLA_V)
    k_nope, v = kv[..., :MLA_NOPE], kv[..., MLA_NOPE:]
    k_pe = apply_rope(k_pe, cos, sin)
    scale = (MLA_NOPE + MLA_ROPE) ** -0.5
    nq = S // Q_BLOCK
    qn_b = q_nope.reshape(B, nq, Q_BLOCK, MLA_HEADS, MLA_NOPE).swapaxes(0, 1)
    qp_b = q_pe.reshape(B, nq, Q_BLOCK, MLA_HEADS, MLA_ROPE).swapaxes(0, 1)

    def block(args):
        qn, qp = args
        s = (jnp.einsum('bqhd,bkhd->bhqk', qn, k_nope)
             + jnp.einsum('bqhr,bkr->bhqk', qp, k_pe))
        p = jax.nn.softmax(s.astype(jnp.float32) * scale, axis=-1).astype(v.dtype)
        return jnp.einsum('bhqk,bkhd->bqhd', p, v)

    o = lax.map(block, (qn_b, qp_b))
    o = o.swapaxes(0, 1).reshape(B, S, MLA_HEADS * MLA_V)
    return o @ w_o


def conv_branch(u, w_dw, b_dw, ln_g, ln_b, w_pw):
    a, gate = jnp.split(u, 2, axis=-1)
    a = a * jax.nn.sigmoid(gate)
    y = lax.conv_general_dilated(
        a, w_dw[:, None, :].astype(a.dtype), window_strides=(1,),
        padding=[(CONV_WIDTH // 2, CONV_WIDTH // 2)],
        dimension_numbers=('NWC', 'WIO', 'NWC'), feature_group_count=CONV_CH) + b_dw
    y = jax.nn.silu(layer_norm(y, ln_g, ln_b))
    return y @ w_pw


def retention_direction(q, k, v, log_gamma, strict):
    C = q.shape[3]
    idx = jnp.arange(C, dtype=jnp.float32)
    diff = idx[:, None] - idx[None, :]
    mask = (diff > 0) if strict else (diff >= 0)
    lg = log_gamma[:, None, None]
    decay = jnp.where(mask, jnp.exp(jnp.where(mask, diff, 0.0) * lg), 0.0).astype(q.dtype)
    scores = jnp.einsum('bhnqd,bhnkd->bhnqk', q, k) * decay[None, :, None]
    inner = jnp.einsum('bhnqk,bhnke->bhnqe', scores, v)
    k_dec = k * jnp.exp((C - 1 - idx)[None, :] * log_gamma[:, None]).astype(k.dtype)[None, :, None, :, None]
    u = jnp.einsum('bhnkd,bhnke->nbhde', k_dec, v)
    chunk_decay = jnp.exp(C * log_gamma).astype(u.dtype)[None, :, None, None]

    def step(state, xs):
        q_c, u_c = xs
        out = jnp.einsum('bhqd,bhde->bhqe', q_c, state)
        return chunk_decay * state + u_c, out

    state0 = jnp.zeros(u.shape[1:], u.dtype)
    _, cross = lax.scan(step, state0, (jnp.moveaxis(q, 2, 0), u))
    cross = jnp.moveaxis(cross, 0, 2) * jnp.exp((idx + 1)[None, :] * log_gamma[:, None]).astype(q.dtype)[None, :, None, :, None]
    return inner + cross


def retention_branch(q, k, v, g, cos, sin, decay_logits, gn_g, w_o):
    B, S, _ = q.shape
    nc = S // RET_CHUNK
    q = apply_rope(q.reshape(B, S, RET_HEADS, RET_DK), cos, sin)
    k = apply_rope(k.reshape(B, S, RET_HEADS, RET_DK), cos, sin) * (RET_DK ** -0.5)
    v = v.reshape(B, S, RET_HEADS, RET_DV)

    def chunked(t):
        return t.reshape(B, nc, RET_CHUNK, RET_HEADS, -1).transpose(0, 3, 1, 2, 4)

    def unchunk(t):
        return t.transpose(0, 2, 3, 1, 4).reshape(B, S, RET_HEADS, -1)

    def flip(t):
        return jnp.flip(t, axis=1)

    log_gamma = jax.nn.log_sigmoid(decay_logits.astype(jnp.float32))
    fwd = retention_direction(chunked(q), chunked(k), chunked(v), log_gamma[0], False)
    bwd = retention_direction(chunked(flip(q)), chunked(flip(k)), chunked(flip(v)), log_gamma[1], True)
    o = unchunk(fwd) + flip(unchunk(bwd))
    of = o.astype(jnp.float32)
    mu = jnp.mean(of, -1, keepdims=True)
    var = jnp.mean(jnp.square(of - mu), -1, keepdims=True)
    on = ((of - mu) * lax.rsqrt(var + EPS)).reshape(B, S, RET_HEADS * RET_DV)
    on = (on * gn_g.astype(jnp.float32)).astype(g.dtype)
    return (jax.nn.silu(g) * on) @ w_o


def setup_inputs(seed: int = 0) -> dict:
    key = jax.random.key(seed)
    ks = jax.random.split(key, 32)
    f32 = jnp.float32

    def w(k, shape, fan_in):
        return jax.random.normal(k, shape, f32) * (fan_in ** -0.5)

    def gain(k, shape):
        return 1.0 + 0.02 * jax.random.normal(k, shape, f32)

    def bias(k, shape):
        return 0.02 * jax.random.normal(k, shape, f32)

    L = DEPTH
    x = jax.random.normal(ks[0], (BATCH, SEQ, D_MODEL), f32)
    offset = jax.random.randint(ks[1], (BATCH, 1), 0, SEQ, dtype=jnp.int32)
    positions = offset + jnp.arange(SEQ, dtype=jnp.int32)[None, :]
    k_exp = 5.0 + jnp.arange(RET_HEADS, dtype=f32)
    base_logit = jnp.log(jnp.exp2(k_exp) - 1.0)
    ret_decay_logits = base_logit[None, None, :] + 0.1 * jax.random.normal(ks[2], (L, 2, RET_HEADS), f32)
    return {
        "x": x,
        "positions": positions,
        "ln_mix_pre": gain(ks[3], (L, D_MODEL)),
        "ln_mix_post": gain(ks[4], (L, D_MODEL)),
        "ln_ffn_pre": gain(ks[5], (L, D_MODEL)),
        "ln_ffn_post": gain(ks[6], (L, D_MODEL)),
        "w_in": w(ks[7], (L, D_MODEL, IN_COLS), D_MODEL),
        "mla_q_norm": gain(ks[8], (L, MLA_Q_RANK)),
        "mla_w_uq": w(ks[9], (L, MLA_Q_RANK, MLA_HEADS * (MLA_NOPE + MLA_ROPE)), MLA_Q_RANK),
        "mla_kv_norm": gain(ks[10], (L, MLA_KV_RANK)),
        "mla_w_ukv": w(ks[11], (L, MLA_KV_RANK, MLA_HEADS * (MLA_NOPE + MLA_V)), MLA_KV_RANK),
        "mla_w_o": w(ks[12], (L, MLA_HEADS * MLA_V, D_MODEL), MLA_HEADS * MLA_V),
        "conv_w_dw": w(ks[13], (L, CONV_WIDTH, CONV_CH), CONV_WIDTH),
        "conv_b_dw": bias(ks[14], (L, CONV_CH)),
        "conv_ln_g": gain(ks[15], (L, CONV_CH)),
        "conv_ln_b": bias(ks[16], (L, CONV_CH)),
        "conv_w_pw": w(ks[17], (L, CONV_CH, D_MODEL), CONV_CH),
        "ret_decay_logits": ret_decay_logits,
        "ret_gn_g": gain(ks[18], (L, RET_HEADS * RET_DV)),
        "ret_w_o": w(ks[19], (L, RET_HEADS * RET_DV, D_MODEL), RET_HEADS * RET_DV),
        "w_out": w(ks[20], (L, D_MODEL, D_MODEL), D_MODEL),
        "ffn_w_gate": w(ks[21], (L, D_MODEL, FFN_HIDDEN), D_MODEL),
        "ffn_w_up": w(ks[22], (L, D_MODEL, FFN_HIDDEN), D_MODEL),
        "ffn_w_down": w(ks[23], (L, FFN_HIDDEN, D_MODEL), FFN_HIDDEN),
    }


def reference(x, positions, ln_mix_pre, ln_mix_post, ln_ffn_pre, ln_ffn_post, w_in,
              mla_q_norm, mla_w_uq, mla_kv_norm, mla_w_ukv, mla_w_o,
              conv_w_dw, conv_b_dw, conv_ln_g, conv_ln_b, conv_w_pw,
              ret_decay_logits, ret_gn_g, ret_w_o, w_out,
              ffn_w_gate, ffn_w_up, ffn_w_down):
    B, S, _ = x.shape
    points = np.cumsum(IN_SPLITS)[:-1].tolist()
    cos_m, sin_m = rope_tables(positions, MLA_ROPE)
    cos_r, sin_r = rope_tables(positions, RET_DK)
    for l in range(DEPTH):
        h = rms_norm(x, ln_mix_pre[l])
        proj = h @ w_in[l]
        c_q, c_kv, k_pe, conv_u, r_q, r_k, r_v, r_g, gate_logits = jnp.split(proj, points, axis=-1)
        y_mla = mla_branch(c_q, c_kv, k_pe, cos_m, sin_m, mla_q_norm[l], mla_w_uq[l],
                           mla_kv_norm[l], mla_w_ukv[l], mla_w_o[l])
        y_conv = conv_branch(conv_u, conv_w_dw[l], conv_b_dw[l], conv_ln_g[l], conv_ln_b[l], conv_w_pw[l])
        y_ret = retention_branch(r_q, r_k, r_v, r_g, cos_r, sin_r, ret_decay_logits[l],
                                 ret_gn_g[l], ret_w_o[l])
        gates = jax.nn.sigmoid(gate_logits).reshape(B, S, N_BRANCH, D_MODEL)
        merged = gates[:, :, 0] * y_mla + gates[:, :, 1] * y_conv + gates[:, :, 2] * y_ret
        x = x + rms_norm(merged @ w_out[l], ln_mix_post[l])
        h = rms_norm(x, ln_ffn_pre[l])
        f = (jax.nn.silu(h @ ffn_w_gate[l]) * (h @ ffn_w_up[l])) @ ffn_w_down[l]
        x = x + rms_norm(f, ln_ffn_post[l])
    return x
```

```python
import functools

import jax
import jax.numpy as jnp
from jax import lax
from jax.experimental import pallas as pl
from jax.experimental.pallas import tpu as pltpu

D_MODEL = 1024
MLA_HEADS = 8
MLA_NOPE = 64
MLA_ROPE = 32
MLA_V = 64
MLA_Q_RANK = 256
MLA_KV_RANK = 128
CONV_CH = 512
CONV_WIDTH = 31
RET_HEADS = 4
RET_DK = 128
RET_DV = 256
RET_CHUNK = 128
FFN_HIDDEN = 2816
N_BRANCH = 3
ROPE_BASE = 10000.0
EPS = 1e-6

LANES = 128
HALO = 16

OFF_GATE = 0
OFF_CQ = OFF_GATE + N_BRANCH * D_MODEL
OFF_CKV = OFF_CQ + MLA_Q_RANK
OFF_CA = OFF_CKV + 2 * LANES
OFF_CG = OFF_CA + CONV_CH
OFF_RQ = OFF_CG + CONV_CH
OFF_RK = OFF_RQ + RET_HEADS * RET_DK
OFF_RV = OFF_RK + RET_HEADS * RET_DK
OFF_RG = OFF_RV + RET_HEADS * RET_DV
PROJ_COLS = OFF_RG + RET_HEADS * RET_DV

VMEM_LIMIT = 56 * 1024 * 1024

f32 = jnp.float32
bf16 = jnp.bfloat16


def _params(*sem):
    return pltpu.CompilerParams(dimension_semantics=sem, vmem_limit_bytes=VMEM_LIMIT)


def _rms(x, g):
    return x * lax.rsqrt(jnp.mean(x * x, axis=-1, keepdims=True) + EPS) * g


def _sigmoid(x):
    return 1.0 / (1.0 + jnp.exp(-x))


def _rope_table_kernel(pos_ref, inv_ref, sign_ref, cm_ref, sm_ref, cr_ref, sr_ref):
    pos = pos_ref[...].astype(f32)
    ang_m = pos * inv_ref[0:1, :]
    ang_r = pos * inv_ref[1:2, :]
    cm_ref[...] = jnp.cos(ang_m)
    sm_ref[...] = jnp.sin(ang_m) * sign_ref[0:1, :]
    cr_ref[...] = jnp.cos(ang_r)
    sr_ref[...] = jnp.sin(ang_r) * sign_ref[1:2, :]


def _rope_tables(positions):
    T = positions.size
    tm = min(2048, T)
    inv_m = ROPE_BASE ** (-jnp.arange(0, MLA_ROPE, 2, dtype=f32) / MLA_ROPE)
    inv_r = ROPE_BASE ** (-jnp.arange(0, RET_DK, 2, dtype=f32) / RET_DK)
    half = MLA_ROPE // 2
    zeros = jnp.zeros
    inv = jnp.stack([
        jnp.concatenate([zeros((MLA_NOPE,), f32), inv_m, inv_m, zeros((LANES - MLA_NOPE - MLA_ROPE,), f32)]),
        jnp.concatenate([inv_r, inv_r])])
    sign = jnp.stack([
        jnp.concatenate([jnp.ones((MLA_NOPE,), f32), -jnp.ones((half,), f32),
                         jnp.ones((LANES - MLA_NOPE - half,), f32)]),
        jnp.concatenate([-jnp.ones((RET_DK // 2,), f32), jnp.ones((RET_DK // 2,), f32)])])
    tab = jax.ShapeDtypeStruct((T, LANES), f32)
    small = pl.BlockSpec((2, LANES), lambda i: (0, 0))
    row = pl.BlockSpec((tm, LANES), lambda i: (i, 0))
    return pl.pallas_call(
        _rope_table_kernel,
        grid=(T // tm,),
        in_specs=[pl.BlockSpec((tm, 1), lambda i: (i, 0)), small, small],
        out_specs=[row, row, row, row],
        out_shape=[tab, tab, tab, tab],
        compiler_params=_params("parallel"),
        name="rope_tables",
    )(positions.reshape(T, 1), inv, sign)


def _inproj_kernel(x_ref, g_ref, w_ref, o_ref, h_ref):
    @pl.when(pl.program_id(1) == 0)
    def _():
        h_ref[...] = _rms(x_ref[...], g_ref[...]).astype(bf16)

    o_ref[...] = jnp.dot(h_ref[...], w_ref[...], preferred_element_type=f32)


def _inproj(x2, g, w):
    T = x2.shape[0]
    tm = min(1024, T)
    tn = 1536
    return pl.pallas_call(
        _inproj_kernel,
        grid=(T // tm, PROJ_COLS // tn),
        in_specs=[pl.BlockSpec((tm, D_MODEL), lambda i, j: (i, 0)),
                  pl.BlockSpec((1, D_MODEL), lambda i, j: (0, 0)),
                  pl.BlockSpec((D_MODEL, tn), lambda i, j: (0, j))],
        out_specs=pl.BlockSpec((tm, tn), lambda i, j: (i, j)),
        out_shape=jax.ShapeDtypeStruct((T, PROJ_COLS), f32),
        scratch_shapes=[pltpu.VMEM((tm, D_MODEL), bf16)],
        compiler_params=_params("parallel", "arbitrary"),
        name="in_proj",
    )(x2, g, w)


def _swap_halves_mla(x, lane):
    return jnp.where(lane < MLA_NOPE + MLA_ROPE // 2,
                     pltpu.roll(x, LANES - MLA_ROPE // 2, 1), pltpu.roll(x, MLA_ROPE // 2, 1))


def _mla_prep_kernel(cq_ref, ckv_ref, cos_ref, sin_ref, qn_ref, kvn_ref, wq_ref, wk_ref, wv_ref,
                     q_out, k_out, v_out):
    tm = cq_ref.shape[0]
    lane = lax.broadcasted_iota(jnp.int32, (tm, LANES), 1)
    cos = cos_ref[...]
    sin = sin_ref[...]
    scale = (MLA_NOPE + MLA_ROPE) ** -0.5

    hq = _rms(cq_ref[...], qn_ref[...]).astype(bf16)
    q = jnp.dot(hq, wq_ref[...], preferred_element_type=f32)
    for h in range(MLA_HEADS):
        qh = q[:, h * LANES:(h + 1) * LANES]
        q_out[0, h] = ((qh * cos + _swap_halves_mla(qh, lane) * sin) * scale).astype(bf16)

    blk = ckv_ref[...]
    hkv = _rms(blk[:, :MLA_KV_RANK], kvn_ref[...]).astype(bf16)
    kpe = blk[:, MLA_KV_RANK:]
    kpe = kpe * cos + _swap_halves_mla(kpe, lane) * sin
    kk = jnp.dot(hkv, wk_ref[...], preferred_element_type=f32)
    vv = jnp.dot(hkv, wv_ref[...], preferred_element_type=f32)
    ones_col = (lane == MLA_V).astype(f32)
    for h in range(MLA_HEADS):
        k_out[0, h] = (kk[:, h * LANES:(h + 1) * LANES] + kpe).astype(bf16)
        v_out[0, h] = (vv[:, h * LANES:(h + 1) * LANES] + ones_col).astype(bf16)


def _mla_prep(proj, cos_m, sin_m, qn, kvn, wq, wk, wv, B, S):
    tm = min(512, S)
    nS = S // tm
    hm = jax.ShapeDtypeStruct((B, MLA_HEADS, S, LANES), bf16)
    hm_spec = pl.BlockSpec((1, MLA_HEADS, tm, LANES), lambda b, i: (b, 0, i, 0))
    full = lambda shape: pl.BlockSpec(shape, lambda b, i: (0, 0))
    return pl.pallas_call(
        _mla_prep_kernel,
        grid=(B, nS),
        in_specs=[pl.BlockSpec((tm, MLA_Q_RANK), lambda b, i: (b * nS + i, OFF_CQ // MLA_Q_RANK)),
                  pl.BlockSpec((tm, 2 * LANES), lambda b, i: (b * nS + i, OFF_CKV // (2 * LANES))),
                  pl.BlockSpec((tm, LANES), lambda b, i: (b * nS + i, 0)),
                  pl.BlockSpec((tm, LANES), lambda b, i: (b * nS + i, 0)),
                  full((1, MLA_Q_RANK)), full((1, MLA_KV_RANK)),
                  full((MLA_Q_RANK, MLA_HEADS * LANES)),
                  full((MLA_KV_RANK, MLA_HEADS * LANES)),
                  full((MLA_KV_RANK, MLA_HEADS * LANES))],
        out_specs=[hm_spec, hm_spec, hm_spec],
        out_shape=[hm, hm, hm],
        compiler_params=_params("parallel", "parallel"),
        name="mla_prep",
    )(proj, proj, cos_m, sin_m, qn, kvn, wq, wk, wv)


def _attn_kernel(q_ref, k_ref, v_ref, o_ref, *, tk):
    tq = q_ref.shape[2]
    S = k_ref.shape[2]
    outs = []
    for hh in range(2):
        q = q_ref[0, hh]

        def body(j, carry):
            m, acc = carry
            start = pl.multiple_of(j * tk, tk)
            ks = k_ref[0, hh, pl.ds(start, tk), :]
            vs = v_ref[0, hh, pl.ds(start, tk), :]
            s = lax.dot_general(q, ks, (((1,), (1,)), ((), ())), preferred_element_type=f32)
            m_new = jnp.maximum(m, jnp.max(s, axis=-1, keepdims=True))
            p = jnp.exp(s - m_new)
            acc = acc * jnp.exp(m - m_new) + jnp.dot(p.astype(bf16), vs, preferred_element_type=f32)
            return m_new, acc

        m0 = jnp.full((tq, 1), -1e30, f32)
        acc0 = jnp.zeros((tq, LANES), f32)
        _, acc = lax.fori_loop(0, S // tk, body, (m0, acc0))
        outs.append(acc / acc[:, MLA_V:MLA_V + 1])
    lane = lax.broadcasted_iota(jnp.int32, (tq, LANES), 1)
    o_ref[0] = jnp.where(lane < MLA_V, outs[0], pltpu.roll(outs[1], MLA_V, 1)).astype(bf16)


def _attention(q, k, v, B, S):
    tq = min(512, S)
    tk = min(512, S)
    hp = MLA_HEADS // 2
    return pl.pallas_call(
        functools.partial(_attn_kernel, tk=tk),
        grid=(B, hp, S // tq),
        in_specs=[pl.BlockSpec((1, 2, tq, LANES), lambda b, h, i: (b, h, i, 0)),
                  pl.BlockSpec((1, 2, S, LANES), lambda b, h, i: (b, h, 0, 0)),
                  pl.BlockSpec((1, 2, S, LANES), lambda b, h, i: (b, h, 0, 0))],
        out_specs=pl.BlockSpec((1, tq, LANES), lambda b, h, i: (b, i, h)),
        out_shape=jax.ShapeDtypeStruct((B, S, MLA_HEADS * MLA_V), bf16),
        compiler_params=_params("parallel", "parallel", "arbitrary"),
        name="mla_attention",
    )(q, k, v)


def _conv_kernel(a_ref, g_ref, ap_ref, gp_ref, an_ref, gn_ref, w_ref, b_ref, lng_ref, lnb_ref,
                 o_ref, ext_ref, *, rows):
    ts = a_ref.shape[0]
    i = pl.program_id(1)
    last = pl.num_programs(1) - 1
    ext_ref[HALO:HALO + ts, :] = a_ref[...] * _sigmoid(g_ref[...])
    prev = ap_ref[...] * _sigmoid(gp_ref[...])
    ext_ref[0:HALO, :] = jnp.where(i > 0, prev, 0.0)
    nxt = an_ref[...] * _sigmoid(gn_ref[...])
    ext_ref[HALO + ts:2 * HALO + ts, :] = jnp.where(i < last, nxt, 0.0)

    w = w_ref[...]
    bias = b_ref[...]
    lng = lng_ref[...]
    lnb = lnb_ref[...]
    first = HALO - CONV_WIDTH // 2
    for c in range(ts // rows):
        r0 = c * rows
        acc = jnp.broadcast_to(bias, (rows, CONV_CH))
        for j in range(CONV_WIDTH):
            acc = acc + ext_ref[r0 + first + j:r0 + first + j + rows, :] * w[j:j + 1, :]
        mu = jnp.mean(acc, axis=-1, keepdims=True)
        d = acc - mu
        var = jnp.mean(d * d, axis=-1, keepdims=True)
        y = d * lax.rsqrt(var + EPS) * lng + lnb
        o_ref[r0:r0 + rows, :] = (y * _sigmoid(y)).astype(bf16)


def _conv_branch(proj, w_dw, b_dw, ln_g, ln_b, B, S):
    ts = min(512, S)
    nS = S // ts
    hb = ts // HALO
    nhb = B * S // HALO
    ca = OFF_CA // CONV_CH
    cg = OFF_CG // CONV_CH

    def main(col):
        return pl.BlockSpec((ts, CONV_CH), lambda b, i: (b * nS + i, col))

    def prev(col):
        return pl.BlockSpec((HALO, CONV_CH), lambda b, i: (jnp.maximum((b * nS + i) * hb - 1, 0), col))

    def nxt(col):
        return pl.BlockSpec((HALO, CONV_CH), lambda b, i: (jnp.minimum((b * nS + i + 1) * hb, nhb - 1), col))

    vec = pl.BlockSpec((1, CONV_CH), lambda b, i: (0, 0))
    return pl.pallas_call(
        functools.partial(_conv_kernel, rows=min(64, ts)),
        grid=(B, nS),
        in_specs=[main(ca), main(cg), prev(ca), prev(cg), nxt(ca), nxt(cg),
                  pl.BlockSpec((CONV_WIDTH, CONV_CH), lambda b, i: (0, 0)), vec, vec, vec],
        out_specs=pl.BlockSpec((ts, CONV_CH), lambda b, i: (b * nS + i, 0)),
        out_shape=jax.ShapeDtypeStruct((B * S, CONV_CH), bf16),
        scratch_shapes=[pltpu.VMEM((ts + 2 * HALO, CONV_CH), f32)],
        compiler_params=_params("parallel", "parallel"),
        name="conv_branch",
    )(proj, proj, proj, proj, proj, proj, w_dw, b_dw, ln_g, ln_b)


def _ret_kernel(q_ref, k_ref, v_ref, g_ref, cos_ref, sin_ref, lg_ref, gn_ref, o_ref,
                qb, qcat, kb, kcat, vb, sall):
    C = RET_CHUNK
    S = q_ref.shape[0]
    NC = S // C
    lgt = lg_ref[0]
    ls = jnp.minimum(lgt, 0.0) - jnp.log(1.0 + jnp.exp(-jnp.abs(lgt)))
    lgf = ls[0:1, :LANES]
    lgb = ls[1:2, :LANES]
    ri = lax.broadcasted_iota(jnp.int32, (C, LANES), 0).astype(f32)
    ci = lax.broadcasted_iota(jnp.int32, (C, LANES), 1).astype(f32)
    diff = ri - ci
    dmat = jnp.where(diff >= 0, jnp.exp(jnp.maximum(diff, 0.0) * lgf), jnp.exp(jnp.maximum(-diff, 0.0) * lgb))
    qf_dec = jnp.exp((ri + 1.0) * lgf)
    qb_dec = jnp.exp((C - ri) * lgb)
    kf_dec = jnp.exp((C - 1.0 - ri) * lgf)
    kb_dec = jnp.exp(ri * lgb)
    cdf = jnp.exp(C * ls[0:1, :])
    cdb = jnp.exp(C * ls[1:2, :])
    kscale = RET_DK ** -0.5
    contract0 = (((0,), (0,)), ((), ()))

    def prep(c, _):
        rows = pl.ds(pl.multiple_of(c * C, C), C)
        cs = cos_ref[rows, :]
        sn = sin_ref[rows, :]
        q = q_ref[rows, :]
        k = k_ref[rows, :]
        qr = q * cs + pltpu.roll(q, RET_DK // 2, 1) * sn
        kr = (k * cs + pltpu.roll(k, RET_DK // 2, 1) * sn) * kscale
        qb[rows, :] = qr.astype(bf16)
        kb[rows, :] = kr.astype(bf16)
        qcat[rows, 0:LANES] = (qr * qf_dec).astype(bf16)
        qcat[rows, LANES:2 * LANES] = (qr * qb_dec).astype(bf16)
        kcat[rows, 0:LANES] = (kr * kf_dec).astype(bf16)
        kcat[rows, LANES:2 * LANES] = (kr * kb_dec).astype(bf16)
        vb[rows, :] = v_ref[rows, :].astype(bf16)
        return 0

    lax.fori_loop(0, NC, prep, 0)

    def bwd(t, sb):
        c = NC - 1 - t
        rows = pl.ds(pl.multiple_of(c * C, C), C)
        sall[c, C:2 * C, :] = sb.astype(bf16)
        upd = lax.dot_general(kcat[rows, LANES:2 * LANES], vb[rows, :], contract0,
                              preferred_element_type=f32)
        return cdb * sb + upd

    lax.fori_loop(0, NC, bwd, jnp.zeros((RET_DK, RET_DV), f32))

    gn = gn_ref[...]

    def fwd(c, sf):
        rows = pl.ds(pl.multiple_of(c * C, C), C)
        sall[c, 0:C, :] = sf.astype(bf16)
        vc = vb[rows, :]
        s = lax.dot_general(qb[rows, :], kb[rows, :], (((1,), (1,)), ((), ())),
                            preferred_element_type=f32) * dmat
        o = jnp.dot(s.astype(bf16), vc, preferred_element_type=f32)
        o = o + jnp.dot(qcat[rows, :], sall[c], preferred_element_type=f32)
        mu = jnp.mean(o, axis=-1, keepdims=True)
        d = o - mu
        var = jnp.mean(d * d, axis=-1, keepdims=True)
        on = d * lax.rsqrt(var + EPS) * gn
        g = g_ref[rows, :]
        o_ref[rows, :] = (g * _sigmoid(g) * on).astype(bf16)
        upd = lax.dot_general(kcat[rows, 0:LANES], vc, contract0, preferred_element_type=f32)
        return cdf * sf + upd

    lax.fori_loop(0, NC, fwd, jnp.zeros((RET_DK, RET_DV), f32))


def _ret_branch(proj, cos_r, sin_r, lg, gn_g, B, S):
    nq = OFF_RQ // RET_DK
    nk = OFF_RK // RET_DK
    nv = OFF_RV // RET_DV
    ng = OFF_RG // RET_DV
    return pl.pallas_call(
        _ret_kernel,
        grid=(B, RET_HEADS),
        in_specs=[pl.BlockSpec((S, RET_DK), lambda b, h: (b, nq + h)),
                  pl.BlockSpec((S, RET_DK), lambda b, h: (b, nk + h)),
                  pl.BlockSpec((S, RET_DV), lambda b, h: (b, nv + h)),
                  pl.BlockSpec((S, RET_DV), lambda b, h: (b, ng + h)),
                  pl.BlockSpec((S, LANES), lambda b, h: (b, 0)),
                  pl.BlockSpec((S, LANES), lambda b, h: (b, 0)),
                  pl.BlockSpec((1, 2, RET_DV), lambda b, h: (h, 0, 0)),
                  pl.BlockSpec((1, RET_DV), lambda b, h: (0, h))],
        out_specs=pl.BlockSpec((S, RET_DV), lambda b, h: (b, h)),
        out_shape=jax.ShapeDtypeStruct((B * S, RET_HEADS * RET_DV), bf16),
        scratch_shapes=[pltpu.VMEM((S, RET_DK), bf16), pltpu.VMEM((S, 2 * RET_DK), bf16),
                        pltpu.VMEM((S, RET_DK), bf16), pltpu.VMEM((S, 2 * RET_DK), bf16),
                        pltpu.VMEM((S, RET_DV), bf16),
                        pltpu.VMEM((S // RET_CHUNK, 2 * RET_DK, RET_DV), bf16)],
        compiler_params=_params("parallel", "parallel"),
        name="retention",
    )(proj, proj, proj, proj, cos_r, sin_r, lg, gn_g)


def _merge_kernel(x_ref, g0_ref, g1_ref, g2_ref, om_ref, oc_ref, or_ref,
                  wm_ref, wc_ref, wr_ref, wo_ref, ln_ref, o_ref):
    y_mla = jnp.dot(om_ref[...], wm_ref[...], preferred_element_type=f32)
    y_conv = jnp.dot(oc_ref[...], wc_ref[...], preferred_element_type=f32)
    y_ret = jnp.dot(or_ref[...], wr_ref[...], preferred_element_type=f32)
    merged = (_sigmoid(g0_ref[...]) * y_mla + _sigmoid(g1_ref[...]) * y_conv
              + _sigmoid(g2_ref[...]) * y_ret)
    z = jnp.dot(merged.astype(bf16), wo_ref[...], preferred_element_type=f32)
    o_ref[...] = x_ref[...] + _rms(z, ln_ref[...])


def _merge(x2, proj, o_mla, o_conv, o_ret, w_mla, w_pw, w_ret, w_out, ln_post):
    T = x2.shape[0]
    tm = min(512, T)
    row = lambda n, col=0: pl.BlockSpec((tm, n), lambda i: (i, col))
    full = lambda r, c: pl.BlockSpec((r, c), lambda i: (0, 0))
    return pl.pallas_call(
        _merge_kernel,
        grid=(T // tm,),
        in_specs=[row(D_MODEL), row(D_MODEL, 0), row(D_MODEL, 1), row(D_MODEL, 2),
                  row(MLA_HEADS * MLA_V), row(CONV_CH), row(RET_HEADS * RET_DV),
                  full(MLA_HEADS * MLA_V, D_MODEL), full(CONV_CH, D_MODEL),
                  full(RET_HEADS * RET_DV, D_MODEL), full(D_MODEL, D_MODEL), full(1, D_MODEL)],
        out_specs=row(D_MODEL),
        out_shape=jax.ShapeDtypeStruct((T, D_MODEL), f32),
        compiler_params=_params("parallel"),
        name="merge_out",
    )(x2, proj, proj, proj, o_mla, o_conv, o_ret, w_mla, w_pw, w_ret, w_out, ln_post)


def _ffn_kernel(x_ref, gpre_ref, wg_ref, wu_ref, wd_ref, gpost_ref, o_ref, h_ref, acc_ref):
    j = pl.program_id(1)

    @pl.when(j == 0)
    def _():
        h_ref[...] = _rms(x_ref[...], gpre_ref[...]).astype(bf16)
        acc_ref[...] = jnp.zeros_like(acc_ref)

    h = h_ref[...]
    a = jnp.dot(h, wg_ref[...], preferred_element_type=f32)
    u = jnp.dot(h, wu_ref[...], preferred_element_type=f32)
    act = (a * _sigmoid(a) * u).astype(bf16)
    acc_ref[...] += jnp.dot(act, wd_ref[...], preferred_element_type=f32)

    @pl.when(j == pl.num_programs(1) - 1)
    def _():
        o_ref[...] = x_ref[...] + _rms(acc_ref[...], gpost_ref[...])


def _ffn(x2, g_pre, wg, wu, wd, g_post):
    T = x2.shape[0]
    tm = min(1024, T)
    th = FFN_HIDDEN // 2
    vec = pl.BlockSpec((1, D_MODEL), lambda i, j: (0, 0))
    return pl.pallas_call(
        _ffn_kernel,
        grid=(T // tm, FFN_HIDDEN // th),
        in_specs=[pl.BlockSpec((tm, D_MODEL), lambda i, j: (i, 0)), vec,
                  pl.BlockSpec((D_MODEL, th), lambda i, j: (0, j)),
                  pl.BlockSpec((D_MODEL, th), lambda i, j: (0, j)),
                  pl.BlockSpec((th, D_MODEL), lambda i, j: (j, 0)), vec],
        out_specs=pl.BlockSpec((tm, D_MODEL), lambda i, j: (i, 0)),
        out_shape=jax.ShapeDtypeStruct((T, D_MODEL), f32),
        scratch_shapes=[pltpu.VMEM((tm, D_MODEL), bf16), pltpu.VMEM((tm, D_MODEL), f32)],
        compiler_params=_params("parallel", "arbitrary"),
        name="ffn",
    )(x2, g_pre, wg, wu, wd, g_post)


def _pad_cols(w, n):
    return jnp.pad(w, ((0, 0), (0, n - w.shape[1])))


def _layout_w_in(w):
    sizes = (MLA_Q_RANK, MLA_KV_RANK, MLA_ROPE, CONV_CH, CONV_CH, RET_HEADS * RET_DK, RET_HEADS * RET_DK,
             RET_HEADS * RET_DV, RET_HEADS * RET_DV, N_BRANCH * D_MODEL)
    pieces = []
    start = 0
    for n in sizes:
        pieces.append(w[:, start:start + n])
        start += n
    c_q, c_kv, k_pe, c_a, c_g, r_q, r_k, r_v, r_g, gates = pieces
    rows = w.shape[0]
    kpe_blk = jnp.concatenate([jnp.zeros((rows, MLA_NOPE), w.dtype), k_pe,
                               jnp.zeros((rows, LANES - MLA_NOPE - MLA_ROPE), w.dtype)], axis=1)
    out = jnp.concatenate([gates, c_q, c_kv, kpe_blk, c_a, c_g, r_q, r_k, r_v, r_g], axis=1)
    assert out.shape[1] == PROJ_COLS
    return out.astype(bf16)


def _layout_heads(w, width):
    rows = w.shape[0]
    w = w.reshape(rows, MLA_HEADS, width)
    w = jnp.pad(w, ((0, 0), (0, 0), (0, LANES - width)))
    return w.reshape(rows, MLA_HEADS * LANES).astype(bf16)


def kernel(x, positions, ln_mix_pre, ln_mix_post, ln_ffn_pre, ln_ffn_post, w_in, mla_q_norm, mla_w_uq,
           mla_kv_norm, mla_w_ukv, mla_w_o, conv_w_dw, conv_b_dw, conv_ln_g, conv_ln_b, conv_w_pw,
           ret_decay_logits, ret_gn_g, ret_w_o, w_out, ffn_w_gate, ffn_w_up, ffn_w_down):
    B, S, _ = x.shape
    depth = w_in.shape[0]
    T = B * S
    cos_m, sin_m, cos_r, sin_r = _rope_tables(positions)
    x2 = x.reshape(T, D_MODEL)
    for l in range(depth):
        row = lambda a: a[l].reshape(1, -1)
        proj = _inproj(x2, row(ln_mix_pre), _layout_w_in(w_in[l]))

        w_ukv = mla_w_ukv[l].reshape(MLA_KV_RANK, MLA_HEADS, MLA_NOPE + MLA_V)
        wk = _layout_heads(w_ukv[:, :, :MLA_NOPE].reshape(MLA_KV_RANK, -1), MLA_NOPE)
        wv = _layout_heads(w_ukv[:, :, MLA_NOPE:].reshape(MLA_KV_RANK, -1), MLA_V)
        wq = _layout_heads(mla_w_uq[l], MLA_NOPE + MLA_ROPE)
        q, k, v = _mla_prep(proj, cos_m, sin_m, row(mla_q_norm), row(mla_kv_norm), wq, wk, wv, B, S)
        o_mla = _attention(q, k, v, B, S).reshape(T, MLA_HEADS * MLA_V)

        o_conv = _conv_branch(proj, conv_w_dw[l], row(conv_b_dw), row(conv_ln_g), row(conv_ln_b), B, S)

        lg = jnp.broadcast_to(ret_decay_logits[l].T[:, :, None], (RET_HEADS, 2, RET_DV))
        o_ret = _ret_branch(proj, cos_r, sin_r, lg, row(ret_gn_g), B, S)

        x2 = _merge(x2, proj, o_mla, o_conv, o_ret, mla_w_o[l].astype(bf16), conv_w_pw[l].astype(bf16),
                    ret_w_o[l].astype(bf16), w_out[l].astype(bf16), row(ln_mix_post))
        x2 = _ffn(x2, row(ln_ffn_pre), ffn_w_gate[l].astype(bf16), ffn_w_up[l].astype(bf16),
                  ffn_w_down[l].astype(bf16), row(ln_ffn_post))
    return x2.reshape(B, S, D_MODEL)
```

```python
import functools

import jax
import jax.numpy as jnp
from jax import lax
from jax.experimental import pallas as pl
from jax.experimental.pallas import tpu as pltpu

D_MODEL = 1024
MLA_HEADS = 8
MLA_NOPE = 64
MLA_ROPE = 32
MLA_V = 64
MLA_Q_RANK = 256
MLA_KV_RANK = 128
CONV_CH = 512
CONV_WIDTH = 31
RET_HEADS = 4
RET_DK = 128
RET_DV = 256
RET_CHUNK = 128
FFN_HIDDEN = 2816
N_BRANCH = 3
ROPE_BASE = 10000.0
EPS = 1e-6
LOG2_E = 1.4426950408889634

LANES = 128
SUBLANES = 8
HALO = 16

OFF_GATE = 0
OFF_CQ = OFF_GATE + N_BRANCH * D_MODEL
OFF_CKV = OFF_CQ + MLA_Q_RANK
OFF_CA = OFF_CKV + 2 * LANES
OFF_CG = OFF_CA + CONV_CH
OFF_RQ = OFF_CG + CONV_CH
OFF_RK = OFF_RQ + RET_HEADS * RET_DK
OFF_RV = OFF_RK + RET_HEADS * RET_DK
OFF_RG = OFF_RV + RET_HEADS * RET_DV
PROJ_COLS = OFF_RG + RET_HEADS * RET_DV

VMEM_LIMIT = 56 * 1024 * 1024

f32 = jnp.float32
bf16 = jnp.bfloat16


def _params(*sem):
    return pltpu.CompilerParams(dimension_semantics=sem, vmem_limit_bytes=VMEM_LIMIT)


def _rms(x, g):
    return x * lax.rsqrt(jnp.mean(x * x, axis=-1, keepdims=True) + EPS) * g


def _sigmoid(x):
    return 1.0 / (1.0 + jnp.exp(-x))


def _rope_table_kernel(pos_ref, inv_ref, sign_ref, cm_ref, sm_ref, cr_ref, sr_ref):
    pos = pos_ref[...].astype(f32)
    ang_m = pos * inv_ref[0:1, :]
    ang_r = pos * inv_ref[1:2, :]
    cm_ref[...] = jnp.cos(ang_m)
    sm_ref[...] = jnp.sin(ang_m) * sign_ref[0:1, :]
    cr_ref[...] = jnp.cos(ang_r)
    sr_ref[...] = jnp.sin(ang_r) * sign_ref[1:2, :]


def _rope_tables(positions):
    T = positions.size
    tm = min(2048, T)
    inv_m = ROPE_BASE ** (-jnp.arange(0, MLA_ROPE, 2, dtype=f32) / MLA_ROPE)
    inv_r = ROPE_BASE ** (-jnp.arange(0, RET_DK, 2, dtype=f32) / RET_DK)
    half = MLA_ROPE // 2
    zeros = jnp.zeros
    inv = jnp.stack([
        jnp.concatenate([zeros((MLA_NOPE,), f32), inv_m, inv_m, zeros((LANES - MLA_NOPE - MLA_ROPE,), f32)]),
        jnp.concatenate([inv_r, inv_r])])
    sign = jnp.stack([
        jnp.concatenate([jnp.ones((MLA_NOPE,), f32), -jnp.ones((half,), f32),
                         jnp.ones((LANES - MLA_NOPE - half,), f32)]),
        jnp.concatenate([-jnp.ones((RET_DK // 2,), f32), jnp.ones((RET_DK // 2,), f32)])])
    tab = jax.ShapeDtypeStruct((T, LANES), f32)
    small = pl.BlockSpec((2, LANES), lambda i: (0, 0))
    row = pl.BlockSpec((tm, LANES), lambda i: (i, 0))
    return pl.pallas_call(
        _rope_table_kernel,
        grid=(T // tm,),
        in_specs=[pl.BlockSpec((tm, 1), lambda i: (i, 0)), small, small],
        out_specs=[row, row, row, row],
        out_shape=[tab, tab, tab, tab],
        compiler_params=_params("parallel"),
        name="rope_tables",
    )(positions.reshape(T, 1), inv, sign)


def _inproj_kernel(x_ref, g_ref, w_ref, o_ref, h_ref):
    @pl.when(pl.program_id(1) == 0)
    def _():
        h_ref[...] = _rms(x_ref[...], g_ref[...]).astype(bf16)

    o_ref[...] = jnp.dot(h_ref[...], w_ref[...], preferred_element_type=f32).astype(o_ref.dtype)


def _inproj(x2, g, w):
    T = x2.shape[0]
    tm = min(1024, T)
    tn = 1536
    return pl.pallas_call(
        _inproj_kernel,
        grid=(T // tm, PROJ_COLS // tn),
        in_specs=[pl.BlockSpec((tm, D_MODEL), lambda i, j: (i, 0)),
                  pl.BlockSpec((1, D_MODEL), lambda i, j: (0, 0)),
                  pl.BlockSpec((D_MODEL, tn), lambda i, j: (0, j))],
        out_specs=pl.BlockSpec((tm, tn), lambda i, j: (i, j)),
        out_shape=jax.ShapeDtypeStruct((T, PROJ_COLS), bf16),
        scratch_shapes=[pltpu.VMEM((tm, D_MODEL), bf16)],
        compiler_params=_params("parallel", "arbitrary"),
        name="in_proj",
    )(x2, g, w)


def _swap_halves_mla(x, lane):
    return jnp.where(lane < MLA_NOPE + MLA_ROPE // 2,
                     pltpu.roll(x, LANES - MLA_ROPE // 2, 1), pltpu.roll(x, MLA_ROPE // 2, 1))


def _mla_prep_kernel(cq_ref, ckv_ref, cos_ref, sin_ref, qn_ref, kvn_ref, wq_ref, wk_ref, wv_ref,
                     q_out, k_out, v_out):
    tm = cq_ref.shape[0]
    lane = lax.broadcasted_iota(jnp.int32, (tm, LANES), 1)
    cos = cos_ref[...]
    sin = sin_ref[...]
    scale = (MLA_NOPE + MLA_ROPE) ** -0.5 * LOG2_E

    hq = _rms(cq_ref[...].astype(f32), qn_ref[...]).astype(bf16)
    q = jnp.dot(hq, wq_ref[...], preferred_element_type=f32)
    for h in range(MLA_HEADS):
        qh = q[:, h * LANES:(h + 1) * LANES]
        q_out[0, h] = ((qh * cos + _swap_halves_mla(qh, lane) * sin) * scale).astype(bf16)

    blk = ckv_ref[...].astype(f32)
    hkv = _rms(blk[:, :MLA_KV_RANK], kvn_ref[...]).astype(bf16)
    kpe = blk[:, MLA_KV_RANK:]
    kpe = kpe * cos + _swap_halves_mla(kpe, lane) * sin
    kk = jnp.dot(hkv, wk_ref[...], preferred_element_type=f32)
    vv = jnp.dot(hkv, wv_ref[...], preferred_element_type=f32)
    ones_col = (lane == MLA_V).astype(f32)
    for h in range(MLA_HEADS):
        k_out[0, h] = (kk[:, h * LANES:(h + 1) * LANES] + kpe).astype(bf16)
        v_out[0, h] = (vv[:, h * LANES:(h + 1) * LANES] + ones_col).astype(bf16)


def _mla_prep(proj, cos_m, sin_m, qn, kvn, wq, wk, wv, B, S):
    tm = min(512, S)
    nS = S // tm
    hm = jax.ShapeDtypeStruct((B, MLA_HEADS, S, LANES), bf16)
    hm_spec = pl.BlockSpec((1, MLA_HEADS, tm, LANES), lambda b, i: (b, 0, i, 0))
    full = lambda shape: pl.BlockSpec(shape, lambda b, i: (0, 0))
    return pl.pallas_call(
        _mla_prep_kernel,
        grid=(B, nS),
        in_specs=[pl.BlockSpec((tm, MLA_Q_RANK), lambda b, i: (b * nS + i, OFF_CQ // MLA_Q_RANK)),
                  pl.BlockSpec((tm, 2 * LANES), lambda b, i: (b * nS + i, OFF_CKV // (2 * LANES))),
                  pl.BlockSpec((tm, LANES), lambda b, i: (b * nS + i, 0)),
                  pl.BlockSpec((tm, LANES), lambda b, i: (b * nS + i, 0)),
                  full((1, MLA_Q_RANK)), full((1, MLA_KV_RANK)),
                  full((MLA_Q_RANK, MLA_HEADS * LANES)),
                  full((MLA_KV_RANK, MLA_HEADS * LANES)),
                  full((MLA_KV_RANK, MLA_HEADS * LANES))],
        out_specs=[hm_spec, hm_spec, hm_spec],
        out_shape=[hm, hm, hm],
        compiler_params=_params("parallel", "parallel"),
        name="mla_prep",
    )(proj, proj, cos_m, sin_m, qn, kvn, wq, wk, wv)


def _attn_kernel(q_ref, k_ref, v_ref, o_ref, *, tk):
    tq = q_ref.shape[2]
    S = k_ref.shape[2]
    heads = q_ref.shape[1]
    qs = [q_ref[0, hh] for hh in range(heads)]

    def body(j, carry):
        start = pl.multiple_of(j * tk, tk)
        new = []
        for hh in range(heads):
            m, acc = carry[hh]
            ks = k_ref[0, hh, pl.ds(start, tk), :]
            vs = v_ref[0, hh, pl.ds(start, tk), :]
            s = lax.dot_general(qs[hh], ks, (((1,), (1,)), ((), ())), preferred_element_type=f32)
            m_new = jnp.maximum(m, jnp.max(s, axis=-1, keepdims=True))
            p = jnp.exp2(s - m_new)
            acc = acc * jnp.exp2(m - m_new) + jnp.dot(p.astype(bf16), vs, preferred_element_type=f32)
            new.append((m_new, acc))
        return tuple(new)

    init = tuple((jnp.full((tq, 1), -1e30, f32), jnp.zeros((tq, LANES), f32)) for _ in range(heads))
    res = lax.fori_loop(0, S // tk, body, init, unroll=True)
    outs = [acc / acc[:, MLA_V:MLA_V + 1] for _, acc in res]
    lane = lax.broadcasted_iota(jnp.int32, (tq, LANES), 1)
    o_ref[0] = jnp.where(lane < MLA_V, outs[0], pltpu.roll(outs[1], MLA_V, 1)).astype(bf16)


def _attention(q, k, v, B, S):
    tq = min(512, S)
    tk = min(2048, S)
    hp = MLA_HEADS // 2
    return pl.pallas_call(
        functools.partial(_attn_kernel, tk=tk),
        grid=(B, hp, S // tq),
        in_specs=[pl.BlockSpec((1, 2, tq, LANES), lambda b, h, i: (b, h, i, 0)),
                  pl.BlockSpec((1, 2, S, LANES), lambda b, h, i: (b, h, 0, 0)),
                  pl.BlockSpec((1, 2, S, LANES), lambda b, h, i: (b, h, 0, 0))],
        out_specs=pl.BlockSpec((1, tq, LANES), lambda b, h, i: (b, i, h)),
        out_shape=jax.ShapeDtypeStruct((B, S, MLA_HEADS * MLA_V), bf16),
        compiler_params=_params("parallel", "parallel", "arbitrary"),
        name="mla_attention",
    )(q, k, v)


def _conv_kernel(a_ref, g_ref, ap_ref, gp_ref, an_ref, gn_ref, w_ref, b_ref, lng_ref, lnb_ref,
                 o_ref, ext_ref, sh_ref, *, rows):
    ts = a_ref.shape[0]
    i = pl.program_id(1)
    last = pl.num_programs(1) - 1

    def glu(a, g):
        return a[...].astype(f32) * _sigmoid(g[...].astype(f32))

    ext_ref[HALO:HALO + ts, :] = glu(a_ref, g_ref)
    prev = glu(ap_ref, gp_ref)
    ext_ref[0:HALO, :] = jnp.where(i > 0, prev, 0.0)
    nxt = glu(an_ref, gn_ref)
    ext_ref[HALO + ts:2 * HALO + ts, :] = jnp.where(i < last, nxt, 0.0)
    n_sh = sh_ref.shape[1]
    for r in range(1, SUBLANES):
        sh_ref[r - 1] = ext_ref[r:r + n_sh, :]

    bias = b_ref[...]
    lng = lng_ref[...]
    lnb = lnb_ref[...]
    first = HALO - CONV_WIDTH // 2
    for c in range(ts // rows):
        r0 = c * rows
        acc = jnp.broadcast_to(bias, (rows // SUBLANES, SUBLANES, CONV_CH))
        for j in range(CONV_WIDTH):
            r = (first + j) % SUBLANES
            base = r0 + first + j - r
            if r == 0:
                tap = ext_ref[base:base + rows, :]
            else:
                tap = sh_ref[r - 1, base:base + rows, :]
            acc = acc + tap.reshape(rows // SUBLANES, SUBLANES, CONV_CH) * w_ref[j]
        acc = acc.reshape(rows, CONV_CH)
        mu = jnp.mean(acc, axis=-1, keepdims=True)
        d = acc - mu
        var = jnp.mean(d * d, axis=-1, keepdims=True)
        y = d * lax.rsqrt(var + EPS) * lng + lnb
        o_ref[r0:r0 + rows, :] = (y * _sigmoid(y)).astype(bf16)


def _conv_branch(proj, w_dw, b_dw, ln_g, ln_b, B, S):
    ts = min(512, S)
    nS = S // ts
    hb = ts // HALO
    nhb = B * S // HALO
    ca = OFF_CA // CONV_CH
    cg = OFF_CG // CONV_CH

    def main(col):
        return pl.BlockSpec((ts, CONV_CH), lambda b, i: (b * nS + i, col))

    def prev(col):
        return pl.BlockSpec((HALO, CONV_CH), lambda b, i: (jnp.maximum((b * nS + i) * hb - 1, 0), col))

    def nxt(col):
        return pl.BlockSpec((HALO, CONV_CH), lambda b, i: (jnp.minimum((b * nS + i + 1) * hb, nhb - 1), col))

    vec = pl.BlockSpec((1, CONV_CH), lambda b, i: (0, 0))
    return pl.pallas_call(
        functools.partial(_conv_kernel, rows=min(64, ts)),
        grid=(B, nS),
        in_specs=[main(ca), main(cg), prev(ca), prev(cg), nxt(ca), nxt(cg),
                  pl.BlockSpec((CONV_WIDTH, SUBLANES, CONV_CH), lambda b, i: (0, 0, 0)), vec, vec, vec],
        out_specs=pl.BlockSpec((ts, CONV_CH), lambda b, i: (b * nS + i, 0)),
        out_shape=jax.ShapeDtypeStruct((B * S, CONV_CH), bf16),
        scratch_shapes=[pltpu.VMEM((ts + 2 * HALO, CONV_CH), f32),
                        pltpu.VMEM((SUBLANES - 1, ts + 2 * HALO - SUBLANES, CONV_CH), f32)],
        compiler_params=_params("parallel", "parallel"),
        name="conv_branch",
    )(proj, proj, proj, proj, proj, proj,
      jnp.broadcast_to(w_dw[:, None, :], (CONV_WIDTH, SUBLANES, CONV_CH)), b_dw, ln_g, ln_b)


def _ret_kernel(q_ref, k_ref, v_ref, g_ref, cos_ref, sin_ref, lg_ref, gn_ref, o_ref,
                kb, sall, sf_ref, sb_ref):
    C = RET_CHUNK
    S = q_ref.shape[0]
    NC = S // C
    lgt = lg_ref[0]
    ls = jnp.minimum(lgt, 0.0) - jnp.log(1.0 + jnp.exp(-jnp.abs(lgt)))
    lgf = ls[0:1, :LANES]
    lgb = ls[1:2, :LANES]
    ri = lax.broadcasted_iota(jnp.int32, (C, LANES), 0).astype(f32)
    ci = lax.broadcasted_iota(jnp.int32, (C, LANES), 1).astype(f32)
    diff = ri - ci
    dmat = jnp.where(diff >= 0, jnp.exp(jnp.maximum(diff, 0.0) * lgf), jnp.exp(jnp.maximum(-diff, 0.0) * lgb))
    qf_dec = jnp.exp((ri + 1.0) * lgf)
    qb_dec = jnp.exp((C - ri) * lgb)
    kf_dec = jnp.exp((C - 1.0 - ri) * lgf)
    kb_dec = jnp.exp(ri * lgb)
    cdf = jnp.exp(C * ls[0:1, :])
    cdb = jnp.exp(C * ls[1:2, :])
    kscale = RET_DK ** -0.5
    contract0 = (((0,), (0,)), ((), ()))

    def chunk_rows(c):
        return pl.ds(pl.multiple_of(c * C, C), C)

    def rope(x, rows):
        x = x.astype(f32)
        return x * cos_ref[rows, :] + pltpu.roll(x, RET_DK // 2, 1) * sin_ref[rows, :]

    sf_ref[...] = jnp.zeros_like(sf_ref)
    sb_ref[...] = jnp.zeros_like(sb_ref)

    def states(t, _):
        cf = t
        cb = NC - 1 - t
        rf = chunk_rows(cf)
        rb = chunk_rows(cb)
        kr = rope(k_ref[rf, :], rf) * kscale
        kb[rf, :] = kr.astype(bf16)
        sall[cf, 0:C, :] = sf_ref[...].astype(bf16)
        sf_ref[...] = cdf * sf_ref[...] + lax.dot_general(
            (kr * kf_dec).astype(bf16), v_ref[rf, :], contract0, preferred_element_type=f32)
        krb = rope(k_ref[rb, :], rb) * kscale
        sall[cb, C:2 * C, :] = sb_ref[...].astype(bf16)
        sb_ref[...] = cdb * sb_ref[...] + lax.dot_general(
            (krb * kb_dec).astype(bf16), v_ref[rb, :], contract0, preferred_element_type=f32)
        return 0

    lax.fori_loop(0, NC, states, 0, unroll=2)

    gn = gn_ref[...]

    def outputs(c, _):
        rows = chunk_rows(c)
        qr = rope(q_ref[rows, :], rows)
        s = lax.dot_general(qr.astype(bf16), kb[rows, :], (((1,), (1,)), ((), ())),
                            preferred_element_type=f32) * dmat
        o = jnp.dot(s.astype(bf16), v_ref[rows, :], preferred_element_type=f32)
        qcat = jnp.concatenate([(qr * qf_dec).astype(bf16), (qr * qb_dec).astype(bf16)], axis=1)
        o = o + jnp.dot(qcat, sall[c], preferred_element_type=f32)
        mu = jnp.mean(o, axis=-1, keepdims=True)
        d = o - mu
        var = jnp.mean(d * d, axis=-1, keepdims=True)
        on = d * lax.rsqrt(var + EPS) * gn
        g = g_ref[rows, :].astype(f32)
        o_ref[rows, :] = (g * _sigmoid(g) * on).astype(bf16)
        return 0

    lax.fori_loop(0, NC, outputs, 0, unroll=4)


def _ret_branch(proj, cos_r, sin_r, lg, gn_g, B, S):
    nq = OFF_RQ // RET_DK
    nk = OFF_RK // RET_DK
    nv = OFF_RV // RET_DV
    ng = OFF_RG // RET_DV
    return pl.pallas_call(
        _ret_kernel,
        grid=(B, RET_HEADS),
        in_specs=[pl.BlockSpec((S, RET_DK), lambda b, h: (b, nq + h)),
                  pl.BlockSpec((S, RET_DK), lambda b, h: (b, nk + h)),
                  pl.BlockSpec((S, RET_DV), lambda b, h: (b, nv + h)),
                  pl.BlockSpec((S, RET_DV), lambda b, h: (b, ng + h)),
                  pl.BlockSpec((S, LANES), lambda b, h: (b, 0)),
                  pl.BlockSpec((S, LANES), lambda b, h: (b, 0)),
                  pl.BlockSpec((1, 2, RET_DV), lambda b, h: (h, 0, 0)),
                  pl.BlockSpec((1, RET_DV), lambda b, h: (0, h))],
        out_specs=pl.BlockSpec((S, RET_DV), lambda b, h: (b, h)),
        out_shape=jax.ShapeDtypeStruct((B * S, RET_HEADS * RET_DV), bf16),
        scratch_shapes=[pltpu.VMEM((S, RET_DK), bf16),
                        pltpu.VMEM((S // RET_CHUNK, 2 * RET_DK, RET_DV), bf16),
                        pltpu.VMEM((RET_DK, RET_DV), f32), pltpu.VMEM((RET_DK, RET_DV), f32)],
        compiler_params=_params("parallel", "parallel"),
        name="retention",
    )(proj, proj, proj, proj, cos_r, sin_r, lg, gn_g)


def _merge_kernel(x_ref, g0_ref, g1_ref, g2_ref, om_ref, oc_ref, or_ref,
                  wm_ref, wc_ref, wr_ref, wo_ref, ln_ref, o_ref):
    y_mla = jnp.dot(om_ref[...], wm_ref[...], preferred_element_type=f32)
    y_conv = jnp.dot(oc_ref[...], wc_ref[...], preferred_element_type=f32)
    y_ret = jnp.dot(or_ref[...], wr_ref[...], preferred_element_type=f32)
    gate = lambda ref: _sigmoid(ref[...].astype(f32))
    merged = gate(g0_ref) * y_mla + gate(g1_ref) * y_conv + gate(g2_ref) * y_ret
    z = jnp.dot(merged.astype(bf16), wo_ref[...], preferred_element_type=f32)
    o_ref[...] = x_ref[...] + _rms(z, ln_ref[...])


def _merge(x2, proj, o_mla, o_conv, o_ret, w_mla, w_pw, w_ret, w_out, ln_post):
    T = x2.shape[0]
    tm = min(512, T)
    row = lambda n, col=0: pl.BlockSpec((tm, n), lambda i: (i, col))
    full = lambda r, c: pl.BlockSpec((r, c), lambda i: (0, 0))
    return pl.pallas_call(
        _merge_kernel,
        grid=(T // tm,),
        in_specs=[row(D_MODEL), row(D_MODEL, 0), row(D_MODEL, 1), row(D_MODEL, 2),
                  row(MLA_HEADS * MLA_V), row(CONV_CH), row(RET_HEADS * RET_DV),
                  full(MLA_HEADS * MLA_V, D_MODEL), full(CONV_CH, D_MODEL),
                  full(RET_HEADS * RET_DV, D_MODEL), full(D_MODEL, D_MODEL), full(1, D_MODEL)],
        out_specs=row(D_MODEL),
        out_shape=jax.ShapeDtypeStruct((T, D_MODEL), f32),
        compiler_params=_params("parallel"),
        name="merge_out",
    )(x2, proj, proj, proj, o_mla, o_conv, o_ret, w_mla, w_pw, w_ret, w_out, ln_post)


def _ffn_kernel(x_ref, gpre_ref, wg_ref, wu_ref, wd_ref, gpost_ref, o_ref):
    x = x_ref[...]
    h = _rms(x, gpre_ref[...]).astype(bf16)
    a = jnp.dot(h, wg_ref[...], preferred_element_type=f32)
    u = jnp.dot(h, wu_ref[...], preferred_element_type=f32)
    act = (a * _sigmoid(a) * u).astype(bf16)
    f = jnp.dot(act, wd_ref[...], preferred_element_type=f32)
    o_ref[...] = x + _rms(f, gpost_ref[...])


def _resident(shape):
    return pl.BlockSpec(shape, lambda i: (0,) * len(shape), pipeline_mode=pl.Buffered(1))


def _ffn(x2, g_pre, wg, wu, wd, g_post):
    T = x2.shape[0]
    tm = min(512, T)
    row = pl.BlockSpec((tm, D_MODEL), lambda i: (i, 0))
    return pl.pallas_call(
        _ffn_kernel,
        grid=(T // tm,),
        in_specs=[row, _resident((1, D_MODEL)), _resident((D_MODEL, FFN_HIDDEN)),
                  _resident((D_MODEL, FFN_HIDDEN)), _resident((FFN_HIDDEN, D_MODEL)),
                  _resident((1, D_MODEL))],
        out_specs=row,
        out_shape=jax.ShapeDtypeStruct((T, D_MODEL), f32),
        compiler_params=_params("parallel"),
        name="ffn",
    )(x2, g_pre, wg, wu, wd, g_post)


def _pad_cols(w, n):
    return jnp.pad(w, ((0, 0), (0, n - w.shape[1])))


def _layout_w_in(w):
    sizes = (MLA_Q_RANK, MLA_KV_RANK, MLA_ROPE, CONV_CH, CONV_CH, RET_HEADS * RET_DK, RET_HEADS * RET_DK,
             RET_HEADS * RET_DV, RET_HEADS * RET_DV, N_BRANCH * D_MODEL)
    pieces = []
    start = 0
    for n in sizes:
        pieces.append(w[:, start:start + n])
        start += n
    c_q, c_kv, k_pe, c_a, c_g, r_q, r_k, r_v, r_g, gates = pieces
    rows = w.shape[0]
    kpe_blk = jnp.concatenate([jnp.zeros((rows, MLA_NOPE), w.dtype), k_pe,
                               jnp.zeros((rows, LANES - MLA_NOPE - MLA_ROPE), w.dtype)], axis=1)
    out = jnp.concatenate([gates, c_q, c_kv, kpe_blk, c_a, c_g, r_q, r_k, r_v, r_g], axis=1)
    assert out.shape[1] == PROJ_COLS
    return out.astype(bf16)


def _layout_heads(w, width):
    rows = w.shape[0]
    w = w.reshape(rows, MLA_HEADS, width)
    w = jnp.pad(w, ((0, 0), (0, 0), (0, LANES - width)))
    return w.reshape(rows, MLA_HEADS * LANES).astype(bf16)


def kernel(x, positions, ln_mix_pre, ln_mix_post, ln_ffn_pre, ln_ffn_post, w_in, mla_q_norm, mla_w_uq,
           mla_kv_norm, mla_w_ukv, mla_w_o, conv_w_dw, conv_b_dw, conv_ln_g, conv_ln_b, conv_w_pw,
           ret_decay_logits, ret_gn_g, ret_w_o, w_out, ffn_w_gate, ffn_w_up, ffn_w_down):
    B, S, _ = x.shape
    depth = w_in.shape[0]
    T = B * S
    cos_m, sin_m, cos_r, sin_r = _rope_tables(positions)
    x2 = x.reshape(T, D_MODEL)
    for l in range(depth):
        row = lambda a: a[l].reshape(1, -1)
        proj = _inproj(x2, row(ln_mix_pre), _layout_w_in(w_in[l]))

        w_ukv = mla_w_ukv[l].reshape(MLA_KV_RANK, MLA_HEADS, MLA_NOPE + MLA_V)
        wk = _layout_heads(w_ukv[:, :, :MLA_NOPE].reshape(MLA_KV_RANK, -1), MLA_NOPE)
        wv = _layout_heads(w_ukv[:, :, MLA_NOPE:].reshape(MLA_KV_RANK, -1), MLA_V)
        wq = _layout_heads(mla_w_uq[l], MLA_NOPE + MLA_ROPE)
        q, k, v = _mla_prep(proj, cos_m, sin_m, row(mla_q_norm), row(mla_kv_norm), wq, wk, wv, B, S)
        o_mla = _attention(q, k, v, B, S).reshape(T, MLA_HEADS * MLA_V)

        o_conv = _conv_branch(proj, conv_w_dw[l], row(conv_b_dw), row(conv_ln_g), row(conv_ln_b), B, S)

        lg = jnp.broadcast_to(ret_decay_logits[l].T[:, :, None], (RET_HEADS, 2, RET_DV))
        o_ret = _ret_branch(proj, cos_r, sin_r, lg, row(ret_gn_g), B, S)

        x2 = _merge(x2, proj, o_mla, o_conv, o_ret, mla_w_o[l].astype(bf16), conv_w_pw[l].astype(bf16),
                    ret_w_o[l].astype(bf16), w_out[l].astype(bf16), row(ln_mix_post))
        x2 = _ffn(x2, row(ln_ffn_pre), ffn_w_gate[l].astype(bf16), ffn_w_up[l].astype(bf16),
                  ffn_w_down[l].astype(bf16), row(ln_ffn_post))
    return x2.reshape(B, S, D_MODEL)
```

```python
import functools

import jax
import jax.numpy as jnp
from jax import lax
from jax.experimental import pallas as pl
from jax.experimental.pallas import tpu as pltpu

D_MODEL = 1024
MLA_HEADS = 8
MLA_NOPE = 64
MLA_ROPE = 32
MLA_V = 64
MLA_Q_RANK = 256
MLA_KV_RANK = 128
CONV_CH = 512
CONV_WIDTH = 31
RET_HEADS = 4
RET_DK = 128
RET_DV = 256
RET_CHUNK = 128
FFN_HIDDEN = 2816
N_BRANCH = 3
ROPE_BASE = 10000.0
EPS = 1e-6
LOG2_E = 1.4426950408889634

LANES = 128
SUBLANES = 8
ATTN_HEADS_PER_STEP = 4
HALO = 16

OFF_GATE = 0
OFF_CQ = OFF_GATE + N_BRANCH * D_MODEL
OFF_CKV = OFF_CQ + MLA_Q_RANK
OFF_RQ = OFF_CKV + 2 * LANES
OFF_RK = OFF_RQ + RET_HEADS * RET_DK
OFF_RV = OFF_RK + RET_HEADS * RET_DK
OFF_RG = OFF_RV + RET_HEADS * RET_DV
PROJ_COLS = OFF_RG + RET_HEADS * RET_DV
CONV_COLS = 2 * CONV_CH
PROJ_CHUNKS = (1792, 1792, 1536, 1536)
assert sum(PROJ_CHUNKS) == PROJ_COLS

VMEM_LIMIT = 56 * 1024 * 1024

f32 = jnp.float32
bf16 = jnp.bfloat16


def _params(*sem):
    return pltpu.CompilerParams(dimension_semantics=sem, vmem_limit_bytes=VMEM_LIMIT)


def _rms(x, g):
    return x * lax.rsqrt(jnp.mean(x * x, axis=-1, keepdims=True) + EPS) * g


def _sigmoid(x):
    return 1.0 / (1.0 + jnp.exp(-x))


def _rope_table_kernel(pos_ref, inv_ref, sign_ref, cm_ref, sm_ref, cr_ref, sr_ref):
    pos = pos_ref[...].astype(f32)
    ang_m = pos * inv_ref[0:1, :]
    ang_r = pos * inv_ref[1:2, :]
    cm_ref[...] = jnp.cos(ang_m)
    sm_ref[...] = jnp.sin(ang_m) * sign_ref[0:1, :]
    cr_ref[...] = jnp.cos(ang_r)
    sr_ref[...] = jnp.sin(ang_r) * sign_ref[1:2, :]


def _rope_tables(positions):
    T = positions.size
    tm = min(2048, T)
    inv_m = ROPE_BASE ** (-jnp.arange(0, MLA_ROPE, 2, dtype=f32) / MLA_ROPE)
    inv_r = ROPE_BASE ** (-jnp.arange(0, RET_DK, 2, dtype=f32) / RET_DK)
    half = MLA_ROPE // 2
    zeros = jnp.zeros
    inv = jnp.stack([
        jnp.concatenate([zeros((MLA_NOPE,), f32), inv_m, inv_m, zeros((LANES - MLA_NOPE - MLA_ROPE,), f32)]),
        jnp.concatenate([inv_r, inv_r])])
    sign = jnp.stack([
        jnp.concatenate([jnp.ones((MLA_NOPE,), f32), -jnp.ones((half,), f32),
                         jnp.ones((LANES - MLA_NOPE - half,), f32)]),
        jnp.concatenate([-jnp.ones((RET_DK // 2,), f32), jnp.ones((RET_DK // 2,), f32)])])
    tab = jax.ShapeDtypeStruct((T, LANES), f32)
    small = pl.BlockSpec((2, LANES), lambda i: (0, 0))
    row = pl.BlockSpec((tm, LANES), lambda i: (i, 0))
    return pl.pallas_call(
        _rope_table_kernel,
        grid=(T // tm,),
        in_specs=[pl.BlockSpec((tm, 1), lambda i: (i, 0)), small, small],
        out_specs=[row, row, row, row],
        out_shape=[tab, tab, tab, tab],
        compiler_params=_params("parallel"),
        name="rope_tables",
    )(positions.reshape(T, 1), inv, sign)


def _resident(shape):
    return pl.BlockSpec(shape, lambda *_: (0,) * len(shape), pipeline_mode=pl.Buffered(1))


def _norm_conv_proj_kernel(x_ref, g_ref, w_ref, h_ref, pc_ref):
    h = _rms(x_ref[...], g_ref[...]).astype(bf16)
    h_ref[...] = h
    pc_ref[...] = jnp.dot(h, w_ref[...], preferred_element_type=f32).astype(pc_ref.dtype)


def _norm_conv_proj(x2, g, w_conv):
    T = x2.shape[0]
    tm = min(1024, T)
    return pl.pallas_call(
        _norm_conv_proj_kernel,
        grid=(T // tm,),
        in_specs=[pl.BlockSpec((tm, D_MODEL), lambda i: (i, 0)), _resident((1, D_MODEL)),
                  _resident((D_MODEL, CONV_COLS))],
        out_specs=[pl.BlockSpec((tm, D_MODEL), lambda i: (i, 0)),
                   pl.BlockSpec((tm, CONV_COLS), lambda i: (i, 0))],
        out_shape=[jax.ShapeDtypeStruct((T, D_MODEL), bf16), jax.ShapeDtypeStruct((T, CONV_COLS), bf16)],
        compiler_params=_params("parallel"),
        name="norm_conv_proj",
    )(x2, g, w_conv)


def _inproj_conv_kernel(h_ref, w_ref, a_ref, g_ref, ap_ref, gp_ref, an_ref, gn_ref,
                        wdw_ref, bdw_ref, lng_ref, lnb_ref, o_ref, oc_ref, ext_ref, sh_ref):
    ts = h_ref.shape[0]
    _conv_prepare(a_ref, g_ref, ap_ref, gp_ref, an_ref, gn_ref, ext_ref, sh_ref,
                  pl.program_id(1), pl.num_programs(1) - 1)
    h = h_ref[...]
    n_conv = ts // min(CONV_ROWS, ts)
    per = -(-n_conv // len(PROJ_CHUNKS))
    col = 0
    for c, width in enumerate(PROJ_CHUNKS):
        o_ref[:, col:col + width] = jnp.dot(h, w_ref[:, col:col + width],
                                            preferred_element_type=f32).astype(o_ref.dtype)
        col += width
        _conv_rows(range(c * per, min((c + 1) * per, n_conv)), wdw_ref, bdw_ref, lng_ref, lnb_ref,
                   oc_ref, ext_ref, sh_ref, ts)


def _inproj_conv(h, w, pc, conv_w, B, S):
    ts = min(512, S)
    nS = S // ts
    w_dw, b_dw, ln_g, ln_b = conv_w
    row = lambda n: pl.BlockSpec((ts, n), lambda b, i: (b * nS + i, 0))
    return pl.pallas_call(
        _inproj_conv_kernel,
        grid=(B, nS),
        in_specs=[row(D_MODEL), _resident((D_MODEL, PROJ_COLS))] + _conv_specs(ts, nS, B * S),
        out_specs=[row(PROJ_COLS), row(CONV_CH)],
        out_shape=[jax.ShapeDtypeStruct((B * S, PROJ_COLS), bf16),
                   jax.ShapeDtypeStruct((B * S, CONV_CH), bf16)],
        scratch_shapes=[pltpu.VMEM((ts + 2 * HALO, CONV_CH), f32),
                        pltpu.VMEM((SUBLANES - 1, ts + 2 * HALO - SUBLANES, CONV_CH), f32)],
        compiler_params=_params("parallel", "parallel"),
        name="in_proj_conv",
    )(h, w, pc, pc, pc, pc, pc, pc,
      jnp.broadcast_to(w_dw[:, None, :], (CONV_WIDTH, SUBLANES, CONV_CH)), b_dw, ln_g, ln_b)


def _swap_halves_mla(x, lane):
    return jnp.where(lane < MLA_NOPE + MLA_ROPE // 2,
                     pltpu.roll(x, LANES - MLA_ROPE // 2, 1), pltpu.roll(x, MLA_ROPE // 2, 1))


def _mla_prep_kernel(cq_ref, ckv_ref, cos_ref, sin_ref, qn_ref, kvn_ref, wq_ref, wk_ref, wv_ref,
                     q_out, k_out, v_out):
    tm = cq_ref.shape[0]
    lane = lax.broadcasted_iota(jnp.int32, (tm, LANES), 1)
    cos = cos_ref[...]
    sin = sin_ref[...]
    scale = (MLA_NOPE + MLA_ROPE) ** -0.5 * LOG2_E

    hq = _rms(cq_ref[...].astype(f32), qn_ref[...]).astype(bf16)
    q = jnp.dot(hq, wq_ref[...], preferred_element_type=f32)
    for h in range(MLA_HEADS):
        qh = q[:, h * LANES:(h + 1) * LANES]
        q_out[0, h] = ((qh * cos + _swap_halves_mla(qh, lane) * sin) * scale).astype(bf16)

    blk = ckv_ref[...].astype(f32)
    hkv = _rms(blk[:, :MLA_KV_RANK], kvn_ref[...]).astype(bf16)
    kpe = blk[:, MLA_KV_RANK:]
    kpe = kpe * cos + _swap_halves_mla(kpe, lane) * sin
    kk = jnp.dot(hkv, wk_ref[...], preferred_element_type=f32)
    vv = jnp.dot(hkv, wv_ref[...], preferred_element_type=f32)
    ones_col = (lane == MLA_V).astype(f32)
    for h in range(MLA_HEADS):
        k_out[0, h] = (kk[:, h * LANES:(h + 1) * LANES] + kpe).astype(bf16)
        v_out[0, h] = (vv[:, h * LANES:(h + 1) * LANES] + ones_col).astype(bf16)


def _mla_prep(proj, cos_m, sin_m, qn, kvn, wq, wk, wv, B, S):
    tm = min(512, S)
    nS = S // tm
    hm = jax.ShapeDtypeStruct((B, MLA_HEADS, S, LANES), bf16)
    hm_spec = pl.BlockSpec((1, MLA_HEADS, tm, LANES), lambda b, i: (b, 0, i, 0))
    full = lambda shape: pl.BlockSpec(shape, lambda b, i: (0, 0))
    return pl.pallas_call(
        _mla_prep_kernel,
        grid=(B, nS),
        in_specs=[pl.BlockSpec((tm, MLA_Q_RANK), lambda b, i: (b * nS + i, OFF_CQ // MLA_Q_RANK)),
                  pl.BlockSpec((tm, 2 * LANES), lambda b, i: (b * nS + i, OFF_CKV // (2 * LANES))),
                  pl.BlockSpec((tm, LANES), lambda b, i: (b * nS + i, 0)),
                  pl.BlockSpec((tm, LANES), lambda b, i: (b * nS + i, 0)),
                  full((1, MLA_Q_RANK)), full((1, MLA_KV_RANK)),
                  full((MLA_Q_RANK, MLA_HEADS * LANES)),
                  full((MLA_KV_RANK, MLA_HEADS * LANES)),
                  full((MLA_KV_RANK, MLA_HEADS * LANES))],
        out_specs=[hm_spec, hm_spec, hm_spec],
        out_shape=[hm, hm, hm],
        compiler_params=_params("parallel", "parallel"),
        name="mla_prep",
    )(proj, proj, cos_m, sin_m, qn, kvn, wq, wk, wv)


def _attn_kernel(q_ref, k_ref, v_ref, o_ref, *, tk):
    tq = q_ref.shape[2]
    S = k_ref.shape[2]
    heads = q_ref.shape[1]
    qs = [q_ref[0, hh] for hh in range(heads)]

    def body(j, carry):
        start = pl.multiple_of(j * tk, tk)
        new = []
        for hh in range(heads):
            m, acc = carry[hh]
            ks = k_ref[0, hh, pl.ds(start, tk), :]
            vs = v_ref[0, hh, pl.ds(start, tk), :]
            s = lax.dot_general(qs[hh], ks, (((1,), (1,)), ((), ())), preferred_element_type=f32)
            m_new = jnp.maximum(m, jnp.max(s, axis=-1, keepdims=True))
            p = jnp.exp2(s - m_new)
            acc = acc * jnp.exp2(m - m_new) + jnp.dot(p.astype(bf16), vs, preferred_element_type=f32)
            new.append((m_new, acc))
        return tuple(new)

    init = tuple((jnp.full((tq, 1), -1e30, f32), jnp.zeros((tq, LANES), f32)) for _ in range(heads))
    res = lax.fori_loop(0, S // tk, body, init, unroll=True)
    outs = [acc / acc[:, MLA_V:MLA_V + 1] for _, acc in res]
    lane = lax.broadcasted_iota(jnp.int32, (tq, LANES), 1)
    for p in range(heads // 2):
        pair = jnp.where(lane < MLA_V, outs[2 * p], pltpu.roll(outs[2 * p + 1], MLA_V, 1))
        o_ref[0, :, p * LANES:(p + 1) * LANES] = pair.astype(bf16)


def _attention(q, k, v, B, S):
    tq = min(512, S)
    tk = min(2048, S)
    hs = ATTN_HEADS_PER_STEP
    return pl.pallas_call(
        functools.partial(_attn_kernel, tk=tk),
        grid=(B, MLA_HEADS // hs, S // tq),
        in_specs=[pl.BlockSpec((1, hs, tq, LANES), lambda b, h, i: (b, h, i, 0)),
                  pl.BlockSpec((1, hs, S, LANES), lambda b, h, i: (b, h, 0, 0)),
                  pl.BlockSpec((1, hs, S, LANES), lambda b, h, i: (b, h, 0, 0))],
        out_specs=pl.BlockSpec((1, tq, hs * MLA_V), lambda b, h, i: (b, i, h)),
        out_shape=jax.ShapeDtypeStruct((B, S, MLA_HEADS * MLA_V), bf16),
        compiler_params=_params("parallel", "parallel", "arbitrary"),
        name="mla_attention",
    )(q, k, v)


CONV_ROWS = 64


def _conv_prepare(a_ref, g_ref, ap_ref, gp_ref, an_ref, gn_ref, ext_ref, sh_ref, i, last):
    ts = a_ref.shape[0]

    def glu(a, g):
        return a[...].astype(f32) * _sigmoid(g[...].astype(f32))

    ext_ref[HALO:HALO + ts, :] = glu(a_ref, g_ref)
    prev = glu(ap_ref, gp_ref)
    ext_ref[0:HALO, :] = jnp.where(i > 0, prev, 0.0)
    nxt = glu(an_ref, gn_ref)
    ext_ref[HALO + ts:2 * HALO + ts, :] = jnp.where(i < last, nxt, 0.0)
    n_sh = sh_ref.shape[1]
    for r in range(1, SUBLANES):
        sh_ref[r - 1] = ext_ref[r:r + n_sh, :]


def _conv_rows(chunks, w_ref, b_ref, lng_ref, lnb_ref, o_ref, ext_ref, sh_ref, ts):
    rows = min(CONV_ROWS, ts)
    bias = b_ref[...]
    lng = lng_ref[...]
    lnb = lnb_ref[...]
    first = HALO - CONV_WIDTH // 2
    for c in chunks:
        r0 = c * rows
        acc = jnp.broadcast_to(bias, (rows // SUBLANES, SUBLANES, CONV_CH))
        for j in range(CONV_WIDTH):
            r = (first + j) % SUBLANES
            base = r0 + first + j - r
            if r == 0:
                tap = ext_ref[base:base + rows, :]
            else:
                tap = sh_ref[r - 1, base:base + rows, :]
            acc = acc + tap.reshape(rows // SUBLANES, SUBLANES, CONV_CH) * w_ref[j]
        acc = acc.reshape(rows, CONV_CH)
        mu = jnp.mean(acc, axis=-1, keepdims=True)
        d = acc - mu
        var = jnp.mean(d * d, axis=-1, keepdims=True)
        y = d * lax.rsqrt(var + EPS) * lng + lnb
        o_ref[r0:r0 + rows, :] = (y * _sigmoid(y)).astype(bf16)


def _conv_specs(ts, nS, n_rows):
    hb = ts // HALO
    nhb = n_rows // HALO
    ca, cg = 0, 1

    def main(col):
        return pl.BlockSpec((ts, CONV_CH), lambda b, i: (b * nS + i, col))

    def prev(col):
        return pl.BlockSpec((HALO, CONV_CH), lambda b, i: (jnp.maximum((b * nS + i) * hb - 1, 0), col))

    def nxt(col):
        return pl.BlockSpec((HALO, CONV_CH), lambda b, i: (jnp.minimum((b * nS + i + 1) * hb, nhb - 1), col))

    vec = pl.BlockSpec((1, CONV_CH), lambda b, i: (0, 0))
    return [main(ca), main(cg), prev(ca), prev(cg), nxt(ca), nxt(cg),
            pl.BlockSpec((CONV_WIDTH, SUBLANES, CONV_CH), lambda b, i: (0, 0, 0)), vec, vec, vec]


def _ret_kernel(q_ref, k_ref, v_ref, g_ref, cos_ref, sin_ref, lg_ref, gn_ref, o_ref,
                kb, sall, sf_ref, sb_ref):
    C = RET_CHUNK
    S = q_ref.shape[0]
    NC = S // C
    lgt = lg_ref[0]
    ls = jnp.minimum(lgt, 0.0) - jnp.log(1.0 + jnp.exp(-jnp.abs(lgt)))
    lgf = ls[0:1, :LANES]
    lgb = ls[1:2, :LANES]
    ri = lax.broadcasted_iota(jnp.int32, (C, LANES), 0).astype(f32)
    ci = lax.broadcasted_iota(jnp.int32, (C, LANES), 1).astype(f32)
    diff = ri - ci
    dmat = jnp.where(diff >= 0, jnp.exp(jnp.maximum(diff, 0.0) * lgf), jnp.exp(jnp.maximum(-diff, 0.0) * lgb))
    qf_dec = jnp.exp((ri + 1.0) * lgf)
    qb_dec = jnp.exp((C - ri) * lgb)
    kf_dec = jnp.exp((C - 1.0 - ri) * lgf)
    kb_dec = jnp.exp(ri * lgb)
    cdf = jnp.exp(C * ls[0:1, :])
    cdb = jnp.exp(C * ls[1:2, :])
    kscale = RET_DK ** -0.5
    contract0 = (((0,), (0,)), ((), ()))

    def chunk_rows(c):
        return pl.ds(pl.multiple_of(c * C, C), C)

    def rope(x, rows):
        x = x.astype(f32)
        return x * cos_ref[rows, :] + pltpu.roll(x, RET_DK // 2, 1) * sin_ref[rows, :]

    sf_ref[...] = jnp.zeros_like(sf_ref)
    sb_ref[...] = jnp.zeros_like(sb_ref)

    def states(t, _):
        cf = t
        cb = NC - 1 - t
        rf = chunk_rows(cf)
        rb = chunk_rows(cb)
        kr = rope(k_ref[rf, :], rf) * kscale
        kb[rf, :] = kr.astype(bf16)
        sall[cf, 0:C, :] = sf_ref[...].astype(bf16)
        sf_ref[...] = cdf * sf_ref[...] + lax.dot_general(
            (kr * kf_dec).astype(bf16), v_ref[rf, :], contract0, preferred_element_type=f32)
        krb = rope(k_ref[rb, :], rb) * kscale
        sall[cb, C:2 * C, :] = sb_ref[...].astype(bf16)
        sb_ref[...] = cdb * sb_ref[...] + lax.dot_general(
            (krb * kb_dec).astype(bf16), v_ref[rb, :], contract0, preferred_element_type=f32)
        return 0

    lax.fori_loop(0, NC, states, 0, unroll=2)

    gn = gn_ref[...]

    def outputs(c, _):
        rows = chunk_rows(c)
        qr = rope(q_ref[rows, :], rows)
        s = lax.dot_general(qr.astype(bf16), kb[rows, :], (((1,), (1,)), ((), ())),
                            preferred_element_type=f32) * dmat
        o = jnp.dot(s.astype(bf16), v_ref[rows, :], preferred_element_type=f32)
        qcat = jnp.concatenate([(qr * qf_dec).astype(bf16), (qr * qb_dec).astype(bf16)], axis=1)
        o = o + jnp.dot(qcat, sall[c], preferred_element_type=f32)
        mu = jnp.mean(o, axis=-1, keepdims=True)
        d = o - mu
        var = jnp.mean(d * d, axis=-1, keepdims=True)
        on = d * lax.rsqrt(var + EPS) * gn
        g = g_ref[rows, :].astype(f32)
        o_ref[rows, :] = (g * _sigmoid(g) * on).astype(bf16)
        return 0

    lax.fori_loop(0, NC, outputs, 0, unroll=8)


def _ret_branch(proj, cos_r, sin_r, lg, gn_g, B, S):
    nq = OFF_RQ // RET_DK
    nk = OFF_RK // RET_DK
    nv = OFF_RV // RET_DV
    ng = OFF_RG // RET_DV
    return pl.pallas_call(
        _ret_kernel,
        grid=(B, RET_HEADS),
        in_specs=[pl.BlockSpec((S, RET_DK), lambda b, h: (b, nq + h)),
                  pl.BlockSpec((S, RET_DK), lambda b, h: (b, nk + h)),
                  pl.BlockSpec((S, RET_DV), lambda b, h: (b, nv + h)),
                  pl.BlockSpec((S, RET_DV), lambda b, h: (b, ng + h)),
                  pl.BlockSpec((S, LANES), lambda b, h: (b, 0)),
                  pl.BlockSpec((S, LANES), lambda b, h: (b, 0)),
                  pl.BlockSpec((1, 2, RET_DV), lambda b, h: (h, 0, 0)),
                  pl.BlockSpec((1, RET_DV), lambda b, h: (0, h))],
        out_specs=pl.BlockSpec((S, RET_DV), lambda b, h: (b, h)),
        out_shape=jax.ShapeDtypeStruct((B * S, RET_HEADS * RET_DV), bf16),
        scratch_shapes=[pltpu.VMEM((S, RET_DK), bf16),
                        pltpu.VMEM((S // RET_CHUNK, 2 * RET_DK, RET_DV), bf16),
                        pltpu.VMEM((RET_DK, RET_DV), f32), pltpu.VMEM((RET_DK, RET_DV), f32)],
        compiler_params=_params("parallel", "parallel"),
        name="retention",
    )(proj, proj, proj, proj, cos_r, sin_r, lg, gn_g)


def _merge_kernel(x_ref, g0_ref, g1_ref, g2_ref, om_ref, oc_ref, or_ref,
                  wm_ref, wc_ref, wr_ref, wo_ref, ln_ref, o_ref):
    gate = lambda ref: _sigmoid(ref[...].astype(f32))
    merged = gate(g0_ref) * jnp.dot(om_ref[...], wm_ref[...], preferred_element_type=f32)
    merged = merged + gate(g1_ref) * jnp.dot(oc_ref[...], wc_ref[...], preferred_element_type=f32)
    merged = merged + gate(g2_ref) * jnp.dot(or_ref[...], wr_ref[...], preferred_element_type=f32)
    z = jnp.dot(merged.astype(bf16), wo_ref[...], preferred_element_type=f32)
    o_ref[...] = x_ref[...] + _rms(z, ln_ref[...])


def _merge(x2, proj, o_mla, o_conv, o_ret, w_mla, w_pw, w_ret, w_out, ln_post):
    T = x2.shape[0]
    tm = min(512, T)
    row = lambda n, col=0: pl.BlockSpec((tm, n), lambda i: (i, col))
    return pl.pallas_call(
        _merge_kernel,
        grid=(T // tm,),
        in_specs=[row(D_MODEL), row(D_MODEL, 0), row(D_MODEL, 1), row(D_MODEL, 2),
                  row(MLA_HEADS * MLA_V), row(CONV_CH), row(RET_HEADS * RET_DV),
                  _resident((MLA_HEADS * MLA_V, D_MODEL)), _resident((CONV_CH, D_MODEL)),
                  _resident((RET_HEADS * RET_DV, D_MODEL)), _resident((D_MODEL, D_MODEL)),
                  _resident((1, D_MODEL))],
        out_specs=row(D_MODEL),
        out_shape=jax.ShapeDtypeStruct((T, D_MODEL), f32),
        compiler_params=_params("parallel"),
        name="merge_out",
    )(x2, proj, proj, proj, o_mla, o_conv, o_ret, w_mla, w_pw, w_ret, w_out, ln_post)


def _ffn_kernel(x_ref, gpre_ref, wg_ref, wu_ref, wd_ref, gpost_ref, o_ref):
    x = x_ref[...]
    h = _rms(x, gpre_ref[...]).astype(bf16)
    a = jnp.dot(h, wg_ref[...], preferred_element_type=f32)
    u = jnp.dot(h, wu_ref[...], preferred_element_type=f32)
    act = (a * _sigmoid(a) * u).astype(bf16)
    f = jnp.dot(act, wd_ref[...], preferred_element_type=f32)
    o_ref[...] = x + _rms(f, gpost_ref[...])


def _ffn(x2, g_pre, wg, wu, wd, g_post):
    T = x2.shape[0]
    tm = min(512, T)
    row = pl.BlockSpec((tm, D_MODEL), lambda i: (i, 0))
    return pl.pallas_call(
        _ffn_kernel,
        grid=(T // tm,),
        in_specs=[row, _resident((1, D_MODEL)), _resident((D_MODEL, FFN_HIDDEN)),
                  _resident((D_MODEL, FFN_HIDDEN)), _resident((FFN_HIDDEN, D_MODEL)),
                  _resident((1, D_MODEL))],
        out_specs=row,
        out_shape=jax.ShapeDtypeStruct((T, D_MODEL), f32),
        compiler_params=_params("parallel"),
        name="ffn",
    )(x2, g_pre, wg, wu, wd, g_post)


def _pad_cols(w, n):
    return jnp.pad(w, ((0, 0), (0, n - w.shape[1])))


def _layout_w_in(w):
    sizes = (MLA_Q_RANK, MLA_KV_RANK, MLA_ROPE, CONV_CH, CONV_CH, RET_HEADS * RET_DK, RET_HEADS * RET_DK,
             RET_HEADS * RET_DV, RET_HEADS * RET_DV, N_BRANCH * D_MODEL)
    pieces = []
    start = 0
    for n in sizes:
        pieces.append(w[:, start:start + n])
        start += n
    c_q, c_kv, k_pe, c_a, c_g, r_q, r_k, r_v, r_g, gates = pieces
    rows = w.shape[0]
    kpe_blk = jnp.concatenate([jnp.zeros((rows, MLA_NOPE), w.dtype), k_pe,
                               jnp.zeros((rows, LANES - MLA_NOPE - MLA_ROPE), w.dtype)], axis=1)
    main = jnp.concatenate([gates, c_q, c_kv, kpe_blk, r_q, r_k, r_v, r_g], axis=1)
    conv = jnp.concatenate([c_a, c_g], axis=1)
    assert main.shape[1] == PROJ_COLS and conv.shape[1] == CONV_COLS
    return main.astype(bf16), conv.astype(bf16)


def _layout_heads(w, width):
    rows = w.shape[0]
    w = w.reshape(rows, MLA_HEADS, width)
    w = jnp.pad(w, ((0, 0), (0, 0), (0, LANES - width)))
    return w.reshape(rows, MLA_HEADS * LANES).astype(bf16)


def kernel(x, positions, ln_mix_pre, ln_mix_post, ln_ffn_pre, ln_ffn_post, w_in, mla_q_norm, mla_w_uq,
           mla_kv_norm, mla_w_ukv, mla_w_o, conv_w_dw, conv_b_dw, conv_ln_g, conv_ln_b, conv_w_pw,
           ret_decay_logits, ret_gn_g, ret_w_o, w_out, ffn_w_gate, ffn_w_up, ffn_w_down):
    B, S, _ = x.shape
    depth = w_in.shape[0]
    T = B * S
    cos_m, sin_m, cos_r, sin_r = _rope_tables(positions)
    x2 = x.reshape(T, D_MODEL)
    for l in range(depth):
        row = lambda a: a[l].reshape(1, -1)
        w_main, w_conv = _layout_w_in(w_in[l])
        h, proj_conv = _norm_conv_proj(x2, row(ln_mix_pre), w_conv)
        conv_w = (conv_w_dw[l], row(conv_b_dw), row(conv_ln_g), row(conv_ln_b))
        proj, o_conv = _inproj_conv(h, w_main, proj_conv, conv_w, B, S)

        w_ukv = mla_w_ukv[l].reshape(MLA_KV_RANK, MLA_HEADS, MLA_NOPE + MLA_V)
        wk = _layout_heads(w_ukv[:, :, :MLA_NOPE].reshape(MLA_KV_RANK, -1), MLA_NOPE)
        wv = _layout_heads(w_ukv[:, :, MLA_NOPE:].reshape(MLA_KV_RANK, -1), MLA_V)
        wq = _layout_heads(mla_w_uq[l], MLA_NOPE + MLA_ROPE)
        q, k, v = _mla_prep(proj, cos_m, sin_m, row(mla_q_norm), row(mla_kv_norm), wq, wk, wv, B, S)
        o_mla = _attention(q, k, v, B, S).reshape(T, MLA_HEADS * MLA_V)

        lg = jnp.broadcast_to(ret_decay_logits[l].T[:, :, None], (RET_HEADS, 2, RET_DV))
        o_ret = _ret_branch(proj, cos_r, sin_r, lg, row(ret_gn_g), B, S)

        x2 = _merge(x2, proj, o_mla, o_conv, o_ret, mla_w_o[l].astype(bf16), conv_w_pw[l].astype(bf16),
                    ret_w_o[l].astype(bf16), w_out[l].astype(bf16), row(ln_mix_post))
        x2 = _ffn(x2, row(ln_ffn_pre), ffn_w_gate[l].astype(bf16), ffn_w_up[l].astype(bf16),
                  ffn_w_down[l].astype(bf16), row(ln_ffn_post))
    return x2.reshape(B, S, D_MODEL)
```

```python
import functools

import jax
import jax.numpy as jnp
from jax import lax
from jax.experimental import pallas as pl
from jax.experimental.pallas import tpu as pltpu

D_MODEL = 1024
MLA_HEADS = 8
MLA_NOPE = 64
MLA_ROPE = 32
MLA_V = 64
MLA_Q_RANK = 256
MLA_KV_RANK = 128
CONV_CH = 512
CONV_WIDTH = 31
RET_HEADS = 4
RET_DK = 128
RET_DV = 256
RET_CHUNK = 128
FFN_HIDDEN = 2816
N_BRANCH = 3
ROPE_BASE = 10000.0
EPS = 1e-6
LOG2_E = 1.4426950408889634

LANES = 128
SUBLANES = 8
ATTN_HEADS_PER_STEP = 2
HALO = 16

OFF_GATE = 0
OFF_CQ = OFF_GATE + N_BRANCH * D_MODEL
OFF_CKV = OFF_CQ + MLA_Q_RANK
OFF_RQ = OFF_CKV + 2 * LANES
OFF_RK = OFF_RQ + RET_HEADS * RET_DK
OFF_RV = OFF_RK + RET_HEADS * RET_DK
OFF_RG = OFF_RV + RET_HEADS * RET_DV
PROJ_COLS = OFF_RG + RET_HEADS * RET_DV
CONV_COLS = 2 * CONV_CH
PROJ_CHUNKS = (1792, 1792, 1536, 1536)
assert sum(PROJ_CHUNKS) == PROJ_COLS

VMEM_LIMIT = 56 * 1024 * 1024

f32 = jnp.float32
bf16 = jnp.bfloat16


def _params(*sem):
    return pltpu.CompilerParams(dimension_semantics=sem, vmem_limit_bytes=VMEM_LIMIT)


def _rms(x, g):
    return x * lax.rsqrt(jnp.mean(x * x, axis=-1, keepdims=True) + EPS) * g


def _sigmoid(x):
    return 1.0 / (1.0 + jnp.exp(-x))


def _rope_table_kernel(pos_ref, inv_ref, sign_ref, cm_ref, sm_ref, cr_ref, sr_ref):
    tm = pos_ref.shape[0]
    pos = pos_ref[...].astype(f32)

    hr = tm // 2
    half = RET_DK // 2
    low = lax.broadcasted_iota(jnp.int32, (hr, LANES), 1) < half
    ang = jnp.where(low, pos[0:hr], pos[hr:tm]) * inv_ref[1:2, :]
    c = jnp.cos(ang)
    s = jnp.sin(ang)
    c_sw = pltpu.roll(c, half, 1)
    s_sw = pltpu.roll(s, half, 1)
    sign_r = sign_ref[1:2, :]
    cr_ref[0:hr, :] = jnp.where(low, c, c_sw)
    cr_ref[hr:tm, :] = jnp.where(low, c_sw, c)
    sr_ref[0:hr, :] = jnp.where(low, s, s_sw) * sign_r
    sr_ref[hr:tm, :] = jnp.where(low, s_sw, s) * sign_r

    nf = MLA_ROPE // 2
    groups = LANES // nf
    gm = tm // groups
    lane = lax.broadcasted_iota(jnp.int32, (gm, LANES), 1)
    grp = lane // nf
    p = pos[(groups - 1) * gm:groups * gm]
    for g in range(groups - 2, -1, -1):
        p = jnp.where(grp == g, pos[g * gm:(g + 1) * gm], p)
    ang = p * inv_ref[0:1, :]
    c = jnp.cos(ang)
    s = jnp.sin(ang)
    first = (lane >= MLA_NOPE) & (lane < MLA_NOPE + nf)
    second = (lane >= MLA_NOPE + nf) & (lane < MLA_NOPE + MLA_ROPE)
    sign_m = sign_ref[0:1, :]

    def place(x, g, fill):
        def shifted(target):
            shift = (target - nf * g) % LANES
            return pltpu.roll(x, shift, 1) if shift else x
        return jnp.where(first, shifted(MLA_NOPE), jnp.where(second, shifted(MLA_NOPE + nf), fill))

    for g in range(groups):
        rows = slice(g * gm, (g + 1) * gm)
        cm_ref[rows, :] = place(c, g, 1.0)
        sm_ref[rows, :] = place(s, g, 0.0) * sign_m


def _rope_tables(positions):
    T = positions.size
    tm = min(2048, T)
    inv_m = ROPE_BASE ** (-jnp.arange(0, MLA_ROPE, 2, dtype=f32) / MLA_ROPE)
    inv_r = ROPE_BASE ** (-jnp.arange(0, RET_DK, 2, dtype=f32) / RET_DK)
    half = MLA_ROPE // 2
    inv = jnp.stack([jnp.tile(inv_m, LANES // half), jnp.concatenate([inv_r, inv_r])])
    sign = jnp.stack([
        jnp.concatenate([jnp.ones((MLA_NOPE,), f32), -jnp.ones((half,), f32),
                         jnp.ones((LANES - MLA_NOPE - half,), f32)]),
        jnp.concatenate([-jnp.ones((RET_DK // 2,), f32), jnp.ones((RET_DK // 2,), f32)])])
    tab = jax.ShapeDtypeStruct((T, LANES), f32)
    small = pl.BlockSpec((2, LANES), lambda i: (0, 0))
    row = pl.BlockSpec((tm, LANES), lambda i: (i, 0))
    return pl.pallas_call(
        _rope_table_kernel,
        grid=(T // tm,),
        in_specs=[pl.BlockSpec((tm, 1), lambda i: (i, 0)), small, small],
        out_specs=[row, row, row, row],
        out_shape=[tab, tab, tab, tab],
        compiler_params=_params("parallel"),
        name="rope_tables",
    )(positions.reshape(T, 1), inv, sign)


def _resident(shape):
    return pl.BlockSpec(shape, lambda *_: (0,) * len(shape), pipeline_mode=pl.Buffered(1))


def _norm_conv_proj_kernel(x_ref, g_ref, w_ref, h_ref, pc_ref):
    h = _rms(x_ref[...], g_ref[...]).astype(bf16)
    h_ref[...] = h
    pc_ref[...] = jnp.dot(h, w_ref[...], preferred_element_type=f32).astype(pc_ref.dtype)


def _norm_conv_proj(x2, g, w_conv):
    T = x2.shape[0]
    tm = min(1024, T)
    return pl.pallas_call(
        _norm_conv_proj_kernel,
        grid=(T // tm,),
        in_specs=[pl.BlockSpec((tm, D_MODEL), lambda i: (i, 0)), _resident((1, D_MODEL)),
                  _resident((D_MODEL, CONV_COLS))],
        out_specs=[pl.BlockSpec((tm, D_MODEL), lambda i: (i, 0)),
                   pl.BlockSpec((tm, CONV_COLS), lambda i: (i, 0))],
        out_shape=[jax.ShapeDtypeStruct((T, D_MODEL), bf16), jax.ShapeDtypeStruct((T, CONV_COLS), bf16)],
        compiler_params=_params("parallel"),
        name="norm_conv_proj",
    )(x2, g, w_conv)


def _inproj_conv_kernel(h_ref, w_ref, a_ref, g_ref, ap_ref, gp_ref, an_ref, gn_ref,
                        wdw_ref, bdw_ref, lng_ref, lnb_ref, o_ref, oc_ref, ext_ref, sh_ref):
    ts = h_ref.shape[0]
    _conv_prepare(a_ref, g_ref, ap_ref, gp_ref, an_ref, gn_ref, ext_ref, sh_ref,
                  pl.program_id(1), pl.num_programs(1) - 1)
    h = h_ref[...]
    rows = min(CONV_ROWS, ts)
    n_conv = ts // rows
    col = 0
    done = 0
    for c, width in enumerate(PROJ_CHUNKS):
        o_ref[:, col:col + width] = jnp.dot(h, w_ref[:, col:col + width],
                                            preferred_element_type=f32).astype(o_ref.dtype)
        col += width
        upto = (c + 1) * n_conv // len(PROJ_CHUNKS)
        for k in range(done, upto):
            _conv_rows(k * rows, rows, wdw_ref, bdw_ref, lng_ref, lnb_ref, oc_ref, ext_ref, sh_ref)
        done = upto


def _inproj_conv(h, w, pc, conv_w, B, S):
    ts = min(512, S)
    nS = S // ts
    w_dw, b_dw, ln_g, ln_b = conv_w
    row = lambda n: pl.BlockSpec((ts, n), lambda b, i: (b * nS + i, 0))
    return pl.pallas_call(
        _inproj_conv_kernel,
        grid=(B, nS),
        in_specs=[row(D_MODEL), _resident((D_MODEL, PROJ_COLS))] + _conv_specs(ts, nS, B * S),
        out_specs=[row(PROJ_COLS), row(CONV_CH)],
        out_shape=[jax.ShapeDtypeStruct((B * S, PROJ_COLS), bf16),
                   jax.ShapeDtypeStruct((B * S, CONV_CH), bf16)],
        scratch_shapes=[pltpu.VMEM((ts + 2 * HALO, CONV_CH), f32),
                        pltpu.VMEM((SUBLANES - 1, ts + 2 * HALO - SUBLANES, CONV_CH), f32)],
        compiler_params=_params("parallel", "parallel"),
        name="in_proj_conv",
    )(h, w, pc, pc, pc, pc, pc, pc,
      jnp.broadcast_to(w_dw[:, None, :], (CONV_WIDTH, SUBLANES, CONV_CH)), b_dw, ln_g, ln_b)


def _swap_halves_mla(x, lane):
    return jnp.where(lane < MLA_NOPE + MLA_ROPE // 2,
                     pltpu.roll(x, LANES - MLA_ROPE // 2, 1), pltpu.roll(x, MLA_ROPE // 2, 1))


def _mla_prep_kernel(cq_ref, ckv_ref, cos_ref, sin_ref, qn_ref, kvn_ref, wq_ref, wk_ref, wv_ref,
                     q_out, k_out, v_out):
    tm = cq_ref.shape[0]
    lane = lax.broadcasted_iota(jnp.int32, (tm, LANES), 1)
    cos = cos_ref[...]
    sin = sin_ref[...]
    scale = (MLA_NOPE + MLA_ROPE) ** -0.5 * LOG2_E

    hq = _rms(cq_ref[...].astype(f32), qn_ref[...]).astype(bf16)
    q = jnp.dot(hq, wq_ref[...], preferred_element_type=f32)
    for h in range(MLA_HEADS):
        qh = q[:, h * LANES:(h + 1) * LANES]
        q_out[0, h] = ((qh * cos + _swap_halves_mla(qh, lane) * sin) * scale).astype(bf16)

    blk = ckv_ref[...].astype(f32)
    hkv = _rms(blk[:, :MLA_KV_RANK], kvn_ref[...]).astype(bf16)
    kpe = blk[:, MLA_KV_RANK:]
    kpe = kpe * cos + _swap_halves_mla(kpe, lane) * sin
    kk = jnp.dot(hkv, wk_ref[...], preferred_element_type=f32)
    vv = jnp.dot(hkv, wv_ref[...], preferred_element_type=f32)
    ones_col = (lane == MLA_V).astype(f32)
    for h in range(MLA_HEADS):
        k_out[0, h] = (kk[:, h * LANES:(h + 1) * LANES] + kpe).astype(bf16)
        v_out[0, h] = (vv[:, h * LANES:(h + 1) * LANES] + ones_col).astype(bf16)


def _mla_prep(proj, cos_m, sin_m, qn, kvn, wq, wk, wv, B, S):
    tm = min(512, S)
    nS = S // tm
    hm = jax.ShapeDtypeStruct((B, MLA_HEADS, S, LANES), bf16)
    hm_spec = pl.BlockSpec((1, MLA_HEADS, tm, LANES), lambda b, i: (b, 0, i, 0))
    full = lambda shape: pl.BlockSpec(shape, lambda b, i: (0, 0))
    return pl.pallas_call(
        _mla_prep_kernel,
        grid=(B, nS),
        in_specs=[pl.BlockSpec((tm, MLA_Q_RANK), lambda b, i: (b * nS + i, OFF_CQ // MLA_Q_RANK)),
                  pl.BlockSpec((tm, 2 * LANES), lambda b, i: (b * nS + i, OFF_CKV // (2 * LANES))),
                  pl.BlockSpec((tm, LANES), lambda b, i: (b * nS + i, 0)),
                  pl.BlockSpec((tm, LANES), lambda b, i: (b * nS + i, 0)),
                  full((1, MLA_Q_RANK)), full((1, MLA_KV_RANK)),
                  full((MLA_Q_RANK, MLA_HEADS * LANES)),
                  full((MLA_KV_RANK, MLA_HEADS * LANES)),
                  full((MLA_KV_RANK, MLA_HEADS * LANES))],
        out_specs=[hm_spec, hm_spec, hm_spec],
        out_shape=[hm, hm, hm],
        compiler_params=_params("parallel", "parallel"),
        name="mla_prep",
    )(proj, proj, cos_m, sin_m, qn, kvn, wq, wk, wv)


def _attn_kernel(q_ref, k_ref, v_ref, o_ref, *, tk):
    tq = q_ref.shape[2]
    S = k_ref.shape[2]
    heads = q_ref.shape[1]
    qs = [q_ref[0, hh] for hh in range(heads)]

    def body(j, carry):
        start = pl.multiple_of(j * tk, tk)
        new = []
        for hh in range(heads):
            m, acc = carry[hh]
            ks = k_ref[0, hh, pl.ds(start, tk), :]
            vs = v_ref[0, hh, pl.ds(start, tk), :]
            s = lax.dot_general(qs[hh], ks, (((1,), (1,)), ((), ())), preferred_element_type=f32)
            m_new = jnp.maximum(m, jnp.max(s, axis=-1, keepdims=True))
            p = jnp.exp2(s - m_new)
            acc = acc * jnp.exp2(m - m_new) + jnp.dot(p.astype(bf16), vs, preferred_element_type=f32)
            new.append((m_new, acc))
        return tuple(new)

    init = tuple((jnp.full((tq, 1), -1e30, f32), jnp.zeros((tq, LANES), f32)) for _ in range(heads))
    res = lax.fori_loop(0, S // tk, body, init, unroll=True)
    outs = [acc / acc[:, MLA_V:MLA_V + 1] for _, acc in res]
    lane = lax.broadcasted_iota(jnp.int32, (tq, LANES), 1)
    for p in range(heads // 2):
        pair = jnp.where(lane < MLA_V, outs[2 * p], pltpu.roll(outs[2 * p + 1], MLA_V, 1))
        o_ref[0, :, p * LANES:(p + 1) * LANES] = pair.astype(bf16)


def _attention(q, k, v, B, S):
    tq = min(1024, S)
    tk = min(2048, S)
    hs = ATTN_HEADS_PER_STEP
    return pl.pallas_call(
        functools.partial(_attn_kernel, tk=tk),
        grid=(B, MLA_HEADS // hs, S // tq),
        in_specs=[pl.BlockSpec((1, hs, tq, LANES), lambda b, h, i: (b, h, i, 0)),
                  pl.BlockSpec((1, hs, S, LANES), lambda b, h, i: (b, h, 0, 0)),
                  pl.BlockSpec((1, hs, S, LANES), lambda b, h, i: (b, h, 0, 0))],
        out_specs=pl.BlockSpec((1, tq, hs * MLA_V), lambda b, h, i: (b, i, h)),
        out_shape=jax.ShapeDtypeStruct((B, S, MLA_HEADS * MLA_V), bf16),
        compiler_params=_params("parallel", "parallel", "arbitrary"),
        name="mla_attention",
    )(q, k, v)


CONV_ROWS = 64


def _conv_prepare(a_ref, g_ref, ap_ref, gp_ref, an_ref, gn_ref, ext_ref, sh_ref, i, last):
    ts = a_ref.shape[0]

    def glu(a, g):
        return a[...].astype(f32) * _sigmoid(g[...].astype(f32))

    ext_ref[HALO:HALO + ts, :] = glu(a_ref, g_ref)
    prev = glu(ap_ref, gp_ref)
    ext_ref[0:HALO, :] = jnp.where(i > 0, prev, 0.0)
    nxt = glu(an_ref, gn_ref)
    ext_ref[HALO + ts:2 * HALO + ts, :] = jnp.where(i < last, nxt, 0.0)
    n_sh = sh_ref.shape[1]
    for r in range(1, SUBLANES):
        sh_ref[r - 1] = ext_ref[r:r + n_sh, :]


def _conv_rows(r0, rows, w_ref, b_ref, lng_ref, lnb_ref, o_ref, ext_ref, sh_ref):
    first = HALO - CONV_WIDTH // 2
    acc = jnp.broadcast_to(b_ref[...], (rows // SUBLANES, SUBLANES, CONV_CH))
    for j in range(CONV_WIDTH):
        r = (first + j) % SUBLANES
        base = pl.ds(r0 + (first + j - r), rows)
        tap = ext_ref[base, :] if r == 0 else sh_ref[r - 1, base, :]
        acc = acc + tap.reshape(rows // SUBLANES, SUBLANES, CONV_CH) * w_ref[j]
    acc = acc.reshape(rows, CONV_CH)
    mu = jnp.mean(acc, axis=-1, keepdims=True)
    d = acc - mu
    var = jnp.mean(d * d, axis=-1, keepdims=True)
    y = d * lax.rsqrt(var + EPS) * lng_ref[...] + lnb_ref[...]
    o_ref[pl.ds(r0, rows), :] = (y * _sigmoid(y)).astype(bf16)


def _conv_specs(ts, nS, n_rows):
    hb = ts // HALO
    nhb = n_rows // HALO
    ca, cg = 0, 1

    def main(col):
        return pl.BlockSpec((ts, CONV_CH), lambda b, i: (b * nS + i, col))

    def prev(col):
        return pl.BlockSpec((HALO, CONV_CH), lambda b, i: (jnp.maximum((b * nS + i) * hb - 1, 0), col))

    def nxt(col):
        return pl.BlockSpec((HALO, CONV_CH), lambda b, i: (jnp.minimum((b * nS + i + 1) * hb, nhb - 1), col))

    vec = pl.BlockSpec((1, CONV_CH), lambda b, i: (0, 0))
    return [main(ca), main(cg), prev(ca), prev(cg), nxt(ca), nxt(cg),
            pl.BlockSpec((CONV_WIDTH, SUBLANES, CONV_CH), lambda b, i: (0, 0, 0)), vec, vec, vec]


def _ret_kernel(q_ref, k_ref, v_ref, g_ref, cos_ref, sin_ref, lg_ref, gn_ref, o_ref,
                kb, sall, sf_ref, sb_ref):
    C = RET_CHUNK
    S = q_ref.shape[0]
    NC = S // C
    lgt = lg_ref[0]
    ls = jnp.minimum(lgt, 0.0) - jnp.log(1.0 + jnp.exp(-jnp.abs(lgt)))
    lgf = ls[0:1, :LANES]
    lgb = ls[1:2, :LANES]
    ri = lax.broadcasted_iota(jnp.int32, (C, LANES), 0).astype(f32)
    ci = lax.broadcasted_iota(jnp.int32, (C, LANES), 1).astype(f32)
    diff = ri - ci
    dmat = jnp.where(diff >= 0, jnp.exp(jnp.maximum(diff, 0.0) * lgf), jnp.exp(jnp.maximum(-diff, 0.0) * lgb))
    qf_dec = jnp.exp((ri + 1.0) * lgf)
    qb_dec = jnp.exp((C - ri) * lgb)
    kf_dec = jnp.exp((C - 1.0 - ri) * lgf)
    kb_dec = jnp.exp(ri * lgb)
    cdf = jnp.exp(C * ls[0:1, :])
    cdb = jnp.exp(C * ls[1:2, :])
    kscale = RET_DK ** -0.5
    contract0 = (((0,), (0,)), ((), ()))

    def chunk_rows(c):
        return pl.ds(pl.multiple_of(c * C, C), C)

    def rope(x, rows):
        x = x.astype(f32)
        return x * cos_ref[rows, :] + pltpu.roll(x, RET_DK // 2, 1) * sin_ref[rows, :]

    sf_ref[...] = jnp.zeros_like(sf_ref)
    sb_ref[...] = jnp.zeros_like(sb_ref)

    def states(t, _):
        cf = t
        cb = NC - 1 - t
        rf = chunk_rows(cf)
        rb = chunk_rows(cb)
        kr = rope(k_ref[rf, :], rf) * kscale
        kb[rf, :] = kr.astype(bf16)
        sall[cf, 0:C, :] = sf_ref[...].astype(bf16)
        sf_ref[...] = cdf * sf_ref[...] + lax.dot_general(
            (kr * kf_dec).astype(bf16), v_ref[rf, :], contract0, preferred_element_type=f32)
        krb = rope(k_ref[rb, :], rb) * kscale
        sall[cb, C:2 * C, :] = sb_ref[...].astype(bf16)
        sb_ref[...] = cdb * sb_ref[...] + lax.dot_general(
            (krb * kb_dec).astype(bf16), v_ref[rb, :], contract0, preferred_element_type=f32)
        return 0

    lax.fori_loop(0, NC, states, 0, unroll=2)

    gn = gn_ref[...]

    def outputs(c, _):
        rows = chunk_rows(c)
        qr = rope(q_ref[rows, :], rows)
        s = lax.dot_general(qr.astype(bf16), kb[rows, :], (((1,), (1,)), ((), ())),
                            preferred_element_type=f32) * dmat
        o = jnp.dot(s.astype(bf16), v_ref[rows, :], preferred_element_type=f32)
        qcat = jnp.concatenate([(qr * qf_dec).astype(bf16), (qr * qb_dec).astype(bf16)], axis=1)
        o = o + jnp.dot(qcat, sall[c], preferred_element_type=f32)
        mu = jnp.mean(o, axis=-1, keepdims=True)
        d = o - mu
        var = jnp.mean(d * d, axis=-1, keepdims=True)
        on = d * lax.rsqrt(var + EPS) * gn
        g = g_ref[rows, :].astype(f32)
        o_ref[rows, :] = (g * _sigmoid(g) * on).astype(bf16)
        return 0

    lax.fori_loop(0, NC, outputs, 0, unroll=8)


def _ret_branch(proj, cos_r, sin_r, lg, gn_g, B, S):
    nq = OFF_RQ // RET_DK
    nk = OFF_RK // RET_DK
    nv = OFF_RV // RET_DV
    ng = OFF_RG // RET_DV
    return pl.pallas_call(
        _ret_kernel,
        grid=(B, RET_HEADS),
        in_specs=[pl.BlockSpec((S, RET_DK), lambda b, h: (b, nq + h)),
                  pl.BlockSpec((S, RET_DK), lambda b, h: (b, nk + h)),
                  pl.BlockSpec((S, RET_DV), lambda b, h: (b, nv + h)),
                  pl.BlockSpec((S, RET_DV), lambda b, h: (b, ng + h)),
                  pl.BlockSpec((S, LANES), lambda b, h: (b, 0)),
                  pl.BlockSpec((S, LANES), lambda b, h: (b, 0)),
                  pl.BlockSpec((1, 2, RET_DV), lambda b, h: (h, 0, 0)),
                  pl.BlockSpec((1, RET_DV), lambda b, h: (0, h))],
        out_specs=pl.BlockSpec((S, RET_DV), lambda b, h: (b, h)),
        out_shape=jax.ShapeDtypeStruct((B * S, RET_HEADS * RET_DV), bf16),
        scratch_shapes=[pltpu.VMEM((S, RET_DK), bf16),
                        pltpu.VMEM((S // RET_CHUNK, 2 * RET_DK, RET_DV), bf16),
                        pltpu.VMEM((RET_DK, RET_DV), f32), pltpu.VMEM((RET_DK, RET_DV), f32)],
        compiler_params=_params("parallel", "parallel"),
        name="retention",
    )(proj, proj, proj, proj, cos_r, sin_r, lg, gn_g)


def _merge_kernel(x_ref, g0_ref, g1_ref, g2_ref, om_ref, oc_ref, or_ref,
                  wm_ref, wc_ref, wr_ref, wo_ref, ln_ref, o_ref):
    gate = lambda ref: _sigmoid(ref[...].astype(f32))
    merged = gate(g0_ref) * jnp.dot(om_ref[...], wm_ref[...], preferred_element_type=f32)
    merged = merged + gate(g1_ref) * jnp.dot(oc_ref[...], wc_ref[...], preferred_element_type=f32)
    merged = merged + gate(g2_ref) * jnp.dot(or_ref[...], wr_ref[...], preferred_element_type=f32)
    z = jnp.dot(merged.astype(bf16), wo_ref[...], preferred_element_type=f32)
    o_ref[...] = x_ref[...] + _rms(z, ln_ref[...])


def _merge(x2, proj, o_mla, o_conv, o_ret, w_mla, w_pw, w_ret, w_out, ln_post):
    T = x2.shape[0]
    tm = min(512, T)
    row = lambda n, col=0: pl.BlockSpec((tm, n), lambda i: (i, col))
    return pl.pallas_call(
        _merge_kernel,
        grid=(T // tm,),
        in_specs=[row(D_MODEL), row(D_MODEL, 0), row(D_MODEL, 1), row(D_MODEL, 2),
                  row(MLA_HEADS * MLA_V), row(CONV_CH), row(RET_HEADS * RET_DV),
                  _resident((MLA_HEADS * MLA_V, D_MODEL)), _resident((CONV_CH, D_MODEL)),
                  _resident((RET_HEADS * RET_DV, D_MODEL)), _resident((D_MODEL, D_MODEL)),
                  _resident((1, D_MODEL))],
        out_specs=row(D_MODEL),
        out_shape=jax.ShapeDtypeStruct((T, D_MODEL), f32),
        compiler_params=_params("parallel"),
        name="merge_out",
    )(x2, proj, proj, proj, o_mla, o_conv, o_ret, w_mla, w_pw, w_ret, w_out, ln_post)


def _ffn_kernel(x_ref, gpre_ref, wg_ref, wu_ref, wd_ref, gpost_ref, o_ref):
    x = x_ref[...]
    h = _rms(x, gpre_ref[...]).astype(bf16)
    a = jnp.dot(h, wg_ref[...], preferred_element_type=f32)
    u = jnp.dot(h, wu_ref[...], preferred_element_type=f32)
    act = (a * _sigmoid(a) * u).astype(bf16)
    f = jnp.dot(act, wd_ref[...], preferred_element_type=f32)
    o_ref[...] = x + _rms(f, gpost_ref[...])


def _ffn(x2, g_pre, wg, wu, wd, g_post):
    T = x2.shape[0]
    tm = min(512, T)
    row = pl.BlockSpec((tm, D_MODEL), lambda i: (i, 0))
    return pl.pallas_call(
        _ffn_kernel,
        grid=(T // tm,),
        in_specs=[row, _resident((1, D_MODEL)), _resident((D_MODEL, FFN_HIDDEN)),
                  _resident((D_MODEL, FFN_HIDDEN)), _resident((FFN_HIDDEN, D_MODEL)),
                  _resident((1, D_MODEL))],
        out_specs=row,
        out_shape=jax.ShapeDtypeStruct((T, D_MODEL), f32),
        compiler_params=_params("parallel"),
        name="ffn",
    )(x2, g_pre, wg, wu, wd, g_post)


def _pad_cols(w, n):
    return jnp.pad(w, ((0, 0), (0, n - w.shape[1])))


def _layout_w_in(w):
    sizes = (MLA_Q_RANK, MLA_KV_RANK, MLA_ROPE, CONV_CH, CONV_CH, RET_HEADS * RET_DK, RET_HEADS * RET_DK,
             RET_HEADS * RET_DV, RET_HEADS * RET_DV, N_BRANCH * D_MODEL)
    pieces = []
    start = 0
    for n in sizes:
        pieces.append(w[:, start:start + n])
        start += n
    c_q, c_kv, k_pe, c_a, c_g, r_q, r_k, r_v, r_g, gates = pieces
    rows = w.shape[0]
    kpe_blk = jnp.concatenate([jnp.zeros((rows, MLA_NOPE), w.dtype), k_pe,
                               jnp.zeros((rows, LANES - MLA_NOPE - MLA_ROPE), w.dtype)], axis=1)
    main = jnp.concatenate([gates, c_q, c_kv, kpe_blk, r_q, r_k, r_v, r_g], axis=1)
    conv = jnp.concatenate([c_a, c_g], axis=1)
    assert main.shape[1] == PROJ_COLS and conv.shape[1] == CONV_COLS
    return main.astype(bf16), conv.astype(bf16)


def _layout_heads(w, width):
    rows = w.shape[0]
    w = w.reshape(rows, MLA_HEADS, width)
    w = jnp.pad(w, ((0, 0), (0, 0), (0, LANES - width)))
    return w.reshape(rows, MLA_HEADS * LANES).astype(bf16)


def kernel(x, positions, ln_mix_pre, ln_mix_post, ln_ffn_pre, ln_ffn_post, w_in, mla_q_norm, mla_w_uq,
           mla_kv_norm, mla_w_ukv, mla_w_o, conv_w_dw, conv_b_dw, conv_ln_g, conv_ln_b, conv_w_pw,
           ret_decay_logits, ret_gn_g, ret_w_o, w_out, ffn_w_gate, ffn_w_up, ffn_w_down):
    B, S, _ = x.shape
    depth = w_in.shape[0]
    T = B * S
    cos_m, sin_m, cos_r, sin_r = _rope_tables(positions)
    x2 = x.reshape(T, D_MODEL)
    for l in range(depth):
        row = lambda a: a[l].reshape(1, -1)
        w_main, w_conv = _layout_w_in(w_in[l])
        h, proj_conv = _norm_conv_proj(x2, row(ln_mix_pre), w_conv)
        conv_w = (conv_w_dw[l], row(conv_b_dw), row(conv_ln_g), row(conv_ln_b))
        proj, o_conv = _inproj_conv(h, w_main, proj_conv, conv_w, B, S)

        w_ukv = mla_w_ukv[l].reshape(MLA_KV_RANK, MLA_HEADS, MLA_NOPE + MLA_V)
        wk = _layout_heads(w_ukv[:, :, :MLA_NOPE].reshape(MLA_KV_RANK, -1), MLA_NOPE)
        wv = _layout_heads(w_ukv[:, :, MLA_NOPE:].reshape(MLA_KV_RANK, -1), MLA_V)
        wq = _layout_heads(mla_w_uq[l], MLA_NOPE + MLA_ROPE)
        q, k, v = _mla_prep(proj, cos_m, sin_m, row(mla_q_norm), row(mla_kv_norm), wq, wk, wv, B, S)
        o_mla = _attention(q, k, v, B, S).reshape(T, MLA_HEADS * MLA_V)

        lg = jnp.broadcast_to(ret_decay_logits[l].T[:, :, None], (RET_HEADS, 2, RET_DV))
        o_ret = _ret_branch(proj, cos_r, sin_r, lg, row(ret_gn_g), B, S)

        x2 = _merge(x2, proj, o_mla, o_conv, o_ret, mla_w_o[l].astype(bf16), conv_w_pw[l].astype(bf16),
                    ret_w_o[l].astype(bf16), w_out[l].astype(bf16), row(ln_mix_post))
        x2 = _ffn(x2, row(ln_ffn_pre), ffn_w_gate[l].astype(bf16), ffn_w_up[l].astype(bf16),
                  ffn_w_down[l].astype(bf16), row(ln_ffn_post))
    return x2.reshape(B, S, D_MODEL)
```

```python
import functools

import jax
import jax.numpy as jnp
from jax import lax
from jax.experimental import pallas as pl
from jax.experimental.pallas import tpu as pltpu

D_MODEL = 1024
MLA_HEADS = 8
MLA_NOPE = 64
MLA_ROPE = 32
MLA_V = 64
MLA_Q_RANK = 256
MLA_KV_RANK = 128
CONV_CH = 512
CONV_WIDTH = 31
RET_HEADS = 4
RET_DK = 128
RET_DV = 256
RET_CHUNK = 128
FFN_HIDDEN = 2816
N_BRANCH = 3
ROPE_BASE = 10000.0
EPS = 1e-6
LOG2_E = 1.4426950408889634

LANES = 128
SUBLANES = 8
ATTN_HEADS_PER_STEP = 2
HALO = 16

OFF_GATE = 0
OFF_CQ = OFF_GATE + N_BRANCH * D_MODEL
OFF_CKV = OFF_CQ + MLA_Q_RANK
OFF_RQ = OFF_CKV + 2 * LANES
OFF_RK = OFF_RQ + RET_HEADS * RET_DK
OFF_RV = OFF_RK + RET_HEADS * RET_DK
OFF_RG = OFF_RV + RET_HEADS * RET_DV
PROJ_COLS = OFF_RG + RET_HEADS * RET_DV
CONV_COLS = 2 * CONV_CH
PROJ_CHUNKS = (1792, 1792, 1536, 1536)
assert sum(PROJ_CHUNKS) == PROJ_COLS

VMEM_LIMIT = 56 * 1024 * 1024

f32 = jnp.float32
bf16 = jnp.bfloat16


def _params(*sem):
    return pltpu.CompilerParams(dimension_semantics=sem, vmem_limit_bytes=VMEM_LIMIT)


def _rms(x, g):
    return x * lax.rsqrt(jnp.mean(x * x, axis=-1, keepdims=True) + EPS) * g


def _sigmoid(x):
    return 1.0 / (1.0 + jnp.exp(-x))


def _rope_table_kernel(pos_ref, inv_ref, sign_ref, cm_ref, sm_ref, cr_ref, sr_ref):
    tm = pos_ref.shape[0]
    pos = pos_ref[...].astype(f32)

    hr = tm // 2
    half = RET_DK // 2
    low = lax.broadcasted_iota(jnp.int32, (hr, LANES), 1) < half
    ang = jnp.where(low, pos[0:hr], pos[hr:tm]) * inv_ref[1:2, :]
    c = jnp.cos(ang)
    s = jnp.sin(ang)
    c_sw = pltpu.roll(c, half, 1)
    s_sw = pltpu.roll(s, half, 1)
    sign_r = sign_ref[1:2, :]
    cr_ref[0:hr, :] = jnp.where(low, c, c_sw)
    cr_ref[hr:tm, :] = jnp.where(low, c_sw, c)
    sr_ref[0:hr, :] = jnp.where(low, s, s_sw) * sign_r
    sr_ref[hr:tm, :] = jnp.where(low, s_sw, s) * sign_r

    nf = MLA_ROPE // 2
    groups = LANES // nf
    gm = tm // groups
    lane = lax.broadcasted_iota(jnp.int32, (gm, LANES), 1)
    grp = lane // nf
    p = pos[(groups - 1) * gm:groups * gm]
    for g in range(groups - 2, -1, -1):
        p = jnp.where(grp == g, pos[g * gm:(g + 1) * gm], p)
    ang = p * inv_ref[0:1, :]
    c = jnp.cos(ang)
    s = jnp.sin(ang)
    first = (lane >= MLA_NOPE) & (lane < MLA_NOPE + nf)
    second = (lane >= MLA_NOPE + nf) & (lane < MLA_NOPE + MLA_ROPE)
    sign_m = sign_ref[0:1, :]

    def place(x, g, fill):
        def shifted(target):
            shift = (target - nf * g) % LANES
            return pltpu.roll(x, shift, 1) if shift else x
        return jnp.where(first, shifted(MLA_NOPE), jnp.where(second, shifted(MLA_NOPE + nf), fill))

    for g in range(groups):
        rows = slice(g * gm, (g + 1) * gm)
        cm_ref[rows, :] = place(c, g, 1.0)
        sm_ref[rows, :] = place(s, g, 0.0) * sign_m


def _rope_tables(positions):
    T = positions.size
    tm = min(2048, T)
    inv_m = ROPE_BASE ** (-jnp.arange(0, MLA_ROPE, 2, dtype=f32) / MLA_ROPE)
    inv_r = ROPE_BASE ** (-jnp.arange(0, RET_DK, 2, dtype=f32) / RET_DK)
    half = MLA_ROPE // 2
    inv = jnp.stack([jnp.tile(inv_m, LANES // half), jnp.concatenate([inv_r, inv_r])])
    sign = jnp.stack([
        jnp.concatenate([jnp.ones((MLA_NOPE,), f32), -jnp.ones((half,), f32),
                         jnp.ones((LANES - MLA_NOPE - half,), f32)]),
        jnp.concatenate([-jnp.ones((RET_DK // 2,), f32), jnp.ones((RET_DK // 2,), f32)])])
    tab = jax.ShapeDtypeStruct((T, LANES), f32)
    small = pl.BlockSpec((2, LANES), lambda i: (0, 0))
    row = pl.BlockSpec((tm, LANES), lambda i: (i, 0))
    return pl.pallas_call(
        _rope_table_kernel,
        grid=(T // tm,),
        in_specs=[pl.BlockSpec((tm, 1), lambda i: (i, 0)), small, small],
        out_specs=[row, row, row, row],
        out_shape=[tab, tab, tab, tab],
        compiler_params=_params("parallel"),
        name="rope_tables",
    )(positions.reshape(T, 1), inv, sign)


def _resident(shape, l):
    return pl.BlockSpec((None,) + shape, lambda *_: (l,) + (0,) * len(shape), pipeline_mode=pl.Buffered(1))


def _norm_conv_proj_kernel(x_ref, g_ref, w_ref, h_ref, pc_ref):
    h = _rms(x_ref[...], g_ref[...]).astype(bf16)
    h_ref[...] = h
    pc_ref[...] = jnp.dot(h, w_ref[...], preferred_element_type=f32).astype(pc_ref.dtype)


def _norm_conv_proj(x2, g, w_conv, l):
    T = x2.shape[0]
    tm = min(1024, T)
    return pl.pallas_call(
        _norm_conv_proj_kernel,
        grid=(T // tm,),
        in_specs=[pl.BlockSpec((tm, D_MODEL), lambda i: (i, 0)), _resident((1, D_MODEL), l),
                  _resident((D_MODEL, CONV_COLS), l)],
        out_specs=[pl.BlockSpec((tm, D_MODEL), lambda i: (i, 0)),
                   pl.BlockSpec((tm, CONV_COLS), lambda i: (i, 0))],
        out_shape=[jax.ShapeDtypeStruct((T, D_MODEL), bf16), jax.ShapeDtypeStruct((T, CONV_COLS), bf16)],
        compiler_params=_params("parallel"),
        name="norm_conv_proj",
    )(x2, g, w_conv)


def _inproj_conv_kernel(h_ref, w_ref, a_ref, g_ref, ap_ref, gp_ref, an_ref, gn_ref,
                        wdw_ref, bdw_ref, lng_ref, lnb_ref, o_ref, oc_ref, ext_ref, sh_ref):
    ts = h_ref.shape[0]
    _conv_prepare(a_ref, g_ref, ap_ref, gp_ref, an_ref, gn_ref, ext_ref, sh_ref,
                  pl.program_id(1), pl.num_programs(1) - 1)
    h = h_ref[...]
    rows = min(CONV_ROWS, ts)
    n_conv = ts // rows
    col = 0
    done = 0
    for c, width in enumerate(PROJ_CHUNKS):
        o_ref[:, col:col + width] = jnp.dot(h, w_ref[:, col:col + width],
                                            preferred_element_type=f32).astype(o_ref.dtype)
        col += width
        upto = (c + 1) * n_conv // len(PROJ_CHUNKS)
        for k in range(done, upto):
            _conv_rows(k * rows, rows, wdw_ref, bdw_ref, lng_ref, lnb_ref, oc_ref, ext_ref, sh_ref)
        done = upto


def _inproj_conv(h, w, pc, conv_w, l, B, S):
    ts = min(512, S)
    nS = S // ts
    w_dw, b_dw, ln_g, ln_b = conv_w
    row = lambda n: pl.BlockSpec((ts, n), lambda b, i: (b * nS + i, 0))
    return pl.pallas_call(
        _inproj_conv_kernel,
        grid=(B, nS),
        in_specs=[row(D_MODEL), _resident((D_MODEL, PROJ_COLS), l)] + _conv_specs(ts, nS, B * S, l),
        out_specs=[row(PROJ_COLS), row(CONV_CH)],
        out_shape=[jax.ShapeDtypeStruct((B * S, PROJ_COLS), bf16),
                   jax.ShapeDtypeStruct((B * S, CONV_CH), bf16)],
        scratch_shapes=[pltpu.VMEM((ts + 2 * HALO, CONV_CH), f32),
                        pltpu.VMEM((SUBLANES - 1, ts + 2 * HALO - SUBLANES, CONV_CH), f32)],
        compiler_params=_params("parallel", "parallel"),
        name="in_proj_conv",
    )(h, w, pc, pc, pc, pc, pc, pc, w_dw, b_dw, ln_g, ln_b)


def _swap_halves_mla(x, lane):
    return jnp.where(lane < MLA_NOPE + MLA_ROPE // 2,
                     pltpu.roll(x, LANES - MLA_ROPE // 2, 1), pltpu.roll(x, MLA_ROPE // 2, 1))


def _mla_prep_kernel(cq_ref, ckv_ref, cos_ref, sin_ref, qn_ref, kvn_ref, wq_ref, wk_ref, wv_ref,
                     q_out, k_out, v_out):
    tm = cq_ref.shape[0]
    lane = lax.broadcasted_iota(jnp.int32, (tm, LANES), 1)
    cos = cos_ref[...]
    sin = sin_ref[...]
    scale = (MLA_NOPE + MLA_ROPE) ** -0.5 * LOG2_E

    hq = _rms(cq_ref[...].astype(f32), qn_ref[...]).astype(bf16)
    q = jnp.dot(hq, wq_ref[...], preferred_element_type=f32)
    for h in range(MLA_HEADS):
        qh = q[:, h * LANES:(h + 1) * LANES]
        q_out[0, h] = ((qh * cos + _swap_halves_mla(qh, lane) * sin) * scale).astype(bf16)

    blk = ckv_ref[...].astype(f32)
    hkv = _rms(blk[:, :MLA_KV_RANK], kvn_ref[...]).astype(bf16)
    kpe = blk[:, MLA_KV_RANK:]
    kpe = kpe * cos + _swap_halves_mla(kpe, lane) * sin
    kk = jnp.dot(hkv, wk_ref[...], preferred_element_type=f32)
    vv = jnp.dot(hkv, wv_ref[...], preferred_element_type=f32)
    ones_col = (lane == MLA_V).astype(f32)
    for h in range(MLA_HEADS):
        k_out[0, h] = (kk[:, h * LANES:(h + 1) * LANES] + kpe).astype(bf16)
        v_out[0, h] = (vv[:, h * LANES:(h + 1) * LANES] + ones_col).astype(bf16)


def _mla_prep(proj, cos_m, sin_m, qn, kvn, wq, wk, wv, l, B, S):
    tm = min(512, S)
    nS = S // tm
    hm = jax.ShapeDtypeStruct((B, MLA_HEADS, S, LANES), bf16)
    hm_spec = pl.BlockSpec((1, MLA_HEADS, tm, LANES), lambda b, i: (b, 0, i, 0))
    full = lambda shape: _resident(shape, l)
    return pl.pallas_call(
        _mla_prep_kernel,
        grid=(B, nS),
        in_specs=[pl.BlockSpec((tm, MLA_Q_RANK), lambda b, i: (b * nS + i, OFF_CQ // MLA_Q_RANK)),
                  pl.BlockSpec((tm, 2 * LANES), lambda b, i: (b * nS + i, OFF_CKV // (2 * LANES))),
                  pl.BlockSpec((tm, LANES), lambda b, i: (b * nS + i, 0)),
                  pl.BlockSpec((tm, LANES), lambda b, i: (b * nS + i, 0)),
                  full((1, MLA_Q_RANK)), full((1, MLA_KV_RANK)),
                  full((MLA_Q_RANK, MLA_HEADS * LANES)),
                  full((MLA_KV_RANK, MLA_HEADS * LANES)),
                  full((MLA_KV_RANK, MLA_HEADS * LANES))],
        out_specs=[hm_spec, hm_spec, hm_spec],
        out_shape=[hm, hm, hm],
        compiler_params=_params("parallel", "parallel"),
        name="mla_prep",
    )(proj, proj, cos_m, sin_m, qn, kvn, wq, wk, wv)


def _attn_kernel(q_ref, k_ref, v_ref, o_ref, *, tk):
    tq = q_ref.shape[2]
    S = k_ref.shape[2]
    heads = q_ref.shape[1]
    qs = [q_ref[0, hh] for hh in range(heads)]

    def body(j, carry):
        start = pl.multiple_of(j * tk, tk)
        new = []
        for hh in range(heads):
            m, acc = carry[hh]
            ks = k_ref[0, hh, pl.ds(start, tk), :]
            vs = v_ref[0, hh, pl.ds(start, tk), :]
            s = lax.dot_general(qs[hh], ks, (((1,), (1,)), ((), ())), preferred_element_type=f32)
            m_new = jnp.maximum(m, jnp.max(s, axis=-1, keepdims=True))
            p = jnp.exp2(s - m_new)
            acc = acc * jnp.exp2(m - m_new) + jnp.dot(p.astype(bf16), vs, preferred_element_type=f32)
            new.append((m_new, acc))
        return tuple(new)

    init = tuple((jnp.full((tq, 1), -1e30, f32), jnp.zeros((tq, LANES), f32)) for _ in range(heads))
    res = lax.fori_loop(0, S // tk, body, init, unroll=True)
    outs = [acc / acc[:, MLA_V:MLA_V + 1] for _, acc in res]
    lane = lax.broadcasted_iota(jnp.int32, (tq, LANES), 1)
    for p in range(heads // 2):
        pair = jnp.where(lane < MLA_V, outs[2 * p], pltpu.roll(outs[2 * p + 1], MLA_V, 1))
        o_ref[0, :, p * LANES:(p + 1) * LANES] = pair.astype(bf16)


def _attention(q, k, v, B, S):
    tq = min(1024, S)
    tk = min(2048, S)
    hs = ATTN_HEADS_PER_STEP
    return pl.pallas_call(
        functools.partial(_attn_kernel, tk=tk),
        grid=(B, MLA_HEADS // hs, S // tq),
        in_specs=[pl.BlockSpec((1, hs, tq, LANES), lambda b, h, i: (b, h, i, 0)),
                  pl.BlockSpec((1, hs, S, LANES), lambda b, h, i: (b, h, 0, 0)),
                  pl.BlockSpec((1, hs, S, LANES), lambda b, h, i: (b, h, 0, 0))],
        out_specs=pl.BlockSpec((1, tq, hs * MLA_V), lambda b, h, i: (b, i, h)),
        out_shape=jax.ShapeDtypeStruct((B, S, MLA_HEADS * MLA_V), bf16),
        compiler_params=_params("parallel", "parallel", "arbitrary"),
        name="mla_attention",
    )(q, k, v)


CONV_ROWS = 64


def _conv_prepare(a_ref, g_ref, ap_ref, gp_ref, an_ref, gn_ref, ext_ref, sh_ref, i, last):
    ts = a_ref.shape[0]

    def glu(a, g):
        return a[...].astype(f32) * _sigmoid(g[...].astype(f32))

    ext_ref[HALO:HALO + ts, :] = glu(a_ref, g_ref)
    prev = glu(ap_ref, gp_ref)
    ext_ref[0:HALO, :] = jnp.where(i > 0, prev, 0.0)
    nxt = glu(an_ref, gn_ref)
    ext_ref[HALO + ts:2 * HALO + ts, :] = jnp.where(i < last, nxt, 0.0)
    n_sh = sh_ref.shape[1]
    for r in range(1, SUBLANES):
        sh_ref[r - 1] = ext_ref[r:r + n_sh, :]


def _conv_rows(r0, rows, w_ref, b_ref, lng_ref, lnb_ref, o_ref, ext_ref, sh_ref):
    first = HALO - CONV_WIDTH // 2
    acc = jnp.broadcast_to(b_ref[...], (rows // SUBLANES, SUBLANES, CONV_CH))
    for j in range(CONV_WIDTH):
        r = (first + j) % SUBLANES
        base = pl.ds(r0 + (first + j - r), rows)
        tap = ext_ref[base, :] if r == 0 else sh_ref[r - 1, base, :]
        acc = acc + tap.reshape(rows // SUBLANES, SUBLANES, CONV_CH) * w_ref[j]
    acc = acc.reshape(rows, CONV_CH)
    mu = jnp.mean(acc, axis=-1, keepdims=True)
    d = acc - mu
    var = jnp.mean(d * d, axis=-1, keepdims=True)
    y = d * lax.rsqrt(var + EPS) * lng_ref[...] + lnb_ref[...]
    o_ref[pl.ds(r0, rows), :] = (y * _sigmoid(y)).astype(bf16)


def _conv_specs(ts, nS, n_rows, l):
    hb = ts // HALO
    nhb = n_rows // HALO
    ca, cg = 0, 1

    def main(col):
        return pl.BlockSpec((ts, CONV_CH), lambda b, i: (b * nS + i, col))

    def prev(col):
        return pl.BlockSpec((HALO, CONV_CH), lambda b, i: (jnp.maximum((b * nS + i) * hb - 1, 0), col))

    def nxt(col):
        return pl.BlockSpec((HALO, CONV_CH), lambda b, i: (jnp.minimum((b * nS + i + 1) * hb, nhb - 1), col))

    vec = _resident((1, CONV_CH), l)
    return [main(ca), main(cg), prev(ca), prev(cg), nxt(ca), nxt(cg),
            _resident((CONV_WIDTH, SUBLANES, CONV_CH), l), vec, vec, vec]


def _ret_kernel(q_ref, k_ref, v_ref, g_ref, cos_ref, sin_ref, lg_ref, gn_ref, o_ref,
                kb, sall, ubuf):
    C = RET_CHUNK
    S = q_ref.shape[0]
    NC = S // C
    lgt = lg_ref[0]
    ls = jnp.minimum(lgt, 0.0) - jnp.log(1.0 + jnp.exp(-jnp.abs(lgt)))
    lgf = ls[0:1, :LANES]
    lgb = ls[1:2, :LANES]
    ri = lax.broadcasted_iota(jnp.int32, (C, LANES), 0).astype(f32)
    ci = lax.broadcasted_iota(jnp.int32, (C, LANES), 1).astype(f32)
    diff = ri - ci
    dmat = jnp.where(diff >= 0, jnp.exp(jnp.maximum(diff, 0.0) * lgf), jnp.exp(jnp.maximum(-diff, 0.0) * lgb))
    qf_dec = jnp.exp((ri + 1.0) * lgf)
    qb_dec = jnp.exp((C - ri) * lgb)
    kf_dec = jnp.exp((C - 1.0 - ri) * lgf)
    kb_dec = jnp.exp(ri * lgb)
    cdf = jnp.exp(C * ls[0:1, :])
    cdb = jnp.exp(C * ls[1:2, :])
    kscale = RET_DK ** -0.5
    contract0 = (((0,), (0,)), ((), ()))

    def chunk_rows(c):
        return pl.ds(pl.multiple_of(c * C, C), C)

    def rope(x, rows):
        x = x.astype(f32)
        return x * cos_ref[rows, :] + pltpu.roll(x, RET_DK // 2, 1) * sin_ref[rows, :]

    def increments(c, _):
        rows = chunk_rows(c)
        kr = rope(k_ref[rows, :], rows) * kscale
        kb[rows, :] = kr.astype(bf16)
        kcat = jnp.concatenate([(kr * kf_dec).astype(bf16), (kr * kb_dec).astype(bf16)], axis=1)
        ubuf[c] = lax.dot_general(kcat, v_ref[rows, :], contract0, preferred_element_type=f32)
        return 0

    lax.fori_loop(0, NC, increments, 0, unroll=8)

    def scan_fwd(c, sf):
        sall[c, 0:C, :] = sf.astype(bf16)
        return cdf * sf + ubuf[c, 0:C, :]

    def scan_bwd(t, sb):
        c = NC - 1 - t
        sall[c, C:2 * C, :] = sb.astype(bf16)
        return cdb * sb + ubuf[c, C:2 * C, :]

    zero_state = jnp.zeros((RET_DK, RET_DV), f32)
    lax.fori_loop(0, NC, scan_fwd, zero_state, unroll=2)
    lax.fori_loop(0, NC, scan_bwd, zero_state, unroll=2)

    gn = gn_ref[...]

    def outputs(c, _):
        rows = chunk_rows(c)
        qr = rope(q_ref[rows, :], rows)
        s = lax.dot_general(qr.astype(bf16), kb[rows, :], (((1,), (1,)), ((), ())),
                            preferred_element_type=f32) * dmat
        o = jnp.dot(s.astype(bf16), v_ref[rows, :], preferred_element_type=f32)
        qcat = jnp.concatenate([(qr * qf_dec).astype(bf16), (qr * qb_dec).astype(bf16)], axis=1)
        o = o + jnp.dot(qcat, sall[c], preferred_element_type=f32)
        mu = jnp.mean(o, axis=-1, keepdims=True)
        d = o - mu
        var = jnp.mean(d * d, axis=-1, keepdims=True)
        on = d * lax.rsqrt(var + EPS) * gn
        g = g_ref[rows, :].astype(f32)
        o_ref[rows, :] = (g * _sigmoid(g) * on).astype(bf16)
        return 0

    lax.fori_loop(0, NC, outputs, 0, unroll=8)


def _ret_branch(proj, cos_r, sin_r, lg, gn_g, l, B, S):
    nq = OFF_RQ // RET_DK
    nk = OFF_RK // RET_DK
    nv = OFF_RV // RET_DV
    ng = OFF_RG // RET_DV
    return pl.pallas_call(
        _ret_kernel,
        grid=(B, RET_HEADS),
        in_specs=[pl.BlockSpec((S, RET_DK), lambda b, h: (b, nq + h)),
                  pl.BlockSpec((S, RET_DK), lambda b, h: (b, nk + h)),
                  pl.BlockSpec((S, RET_DV), lambda b, h: (b, nv + h)),
                  pl.BlockSpec((S, RET_DV), lambda b, h: (b, ng + h)),
                  pl.BlockSpec((S, LANES), lambda b, h: (b, 0)),
                  pl.BlockSpec((S, LANES), lambda b, h: (b, 0)),
                  pl.BlockSpec((None, 1, 2, RET_DV), lambda b, h: (l, h, 0, 0)),
                  pl.BlockSpec((None, 1, RET_DV), lambda b, h: (l, 0, h))],
        out_specs=pl.BlockSpec((S, RET_DV), lambda b, h: (b, h)),
        out_shape=jax.ShapeDtypeStruct((B * S, RET_HEADS * RET_DV), bf16),
        scratch_shapes=[pltpu.VMEM((S, RET_DK), bf16),
                        pltpu.VMEM((S // RET_CHUNK, 2 * RET_DK, RET_DV), bf16),
                        pltpu.VMEM((S // RET_CHUNK, 2 * RET_DK, RET_DV), f32)],
        compiler_params=_params("parallel", "parallel"),
        name="retention",
    )(proj, proj, proj, proj, cos_r, sin_r, lg, gn_g)


def _merge_kernel(x_ref, g0_ref, g1_ref, g2_ref, om_ref, oc_ref, or_ref,
                  wm_ref, wc_ref, wr_ref, wo_ref, ln_ref, o_ref):
    gate = lambda ref: _sigmoid(ref[...].astype(f32))
    merged = gate(g0_ref) * jnp.dot(om_ref[...], wm_ref[...], preferred_element_type=f32)
    merged = merged + gate(g1_ref) * jnp.dot(oc_ref[...], wc_ref[...], preferred_element_type=f32)
    merged = merged + gate(g2_ref) * jnp.dot(or_ref[...], wr_ref[...], preferred_element_type=f32)
    z = jnp.dot(merged.astype(bf16), wo_ref[...], preferred_element_type=f32)
    o_ref[...] = x_ref[...] + _rms(z, ln_ref[...])


def _merge(x2, proj, o_mla, o_conv, o_ret, w_mla, w_pw, w_ret, w_out, ln_post, l):
    T = x2.shape[0]
    tm = min(512, T)
    row = lambda n, col=0: pl.BlockSpec((tm, n), lambda i: (i, col))
    return pl.pallas_call(
        _merge_kernel,
        grid=(T // tm,),
        in_specs=[row(D_MODEL), row(D_MODEL, 0), row(D_MODEL, 1), row(D_MODEL, 2),
                  row(MLA_HEADS * MLA_V), row(CONV_CH), row(RET_HEADS * RET_DV),
                  _resident((MLA_HEADS * MLA_V, D_MODEL), l), _resident((CONV_CH, D_MODEL), l),
                  _resident((RET_HEADS * RET_DV, D_MODEL), l), _resident((D_MODEL, D_MODEL), l),
                  _resident((1, D_MODEL), l)],
        out_specs=row(D_MODEL),
        out_shape=jax.ShapeDtypeStruct((T, D_MODEL), f32),
        compiler_params=_params("parallel"),
        name="merge_out",
    )(x2, proj, proj, proj, o_mla, o_conv, o_ret, w_mla, w_pw, w_ret, w_out, ln_post)


def _ffn_kernel(x_ref, gpre_ref, wg_ref, wu_ref, wd_ref, gpost_ref, o_ref):
    x = x_ref[...]
    h = _rms(x, gpre_ref[...]).astype(bf16)
    a = jnp.dot(h, wg_ref[...], preferred_element_type=f32)
    u = jnp.dot(h, wu_ref[...], preferred_element_type=f32)
    act = (a * _sigmoid(a) * u).astype(bf16)
    f = jnp.dot(act, wd_ref[...], preferred_element_type=f32)
    o_ref[...] = x + _rms(f, gpost_ref[...])


def _ffn(x2, g_pre, wg, wu, wd, g_post, l):
    T = x2.shape[0]
    tm = min(512, T)
    row = pl.BlockSpec((tm, D_MODEL), lambda i: (i, 0))
    return pl.pallas_call(
        _ffn_kernel,
        grid=(T // tm,),
        in_specs=[row, _resident((1, D_MODEL), l), _resident((D_MODEL, FFN_HIDDEN), l),
                  _resident((D_MODEL, FFN_HIDDEN), l), _resident((FFN_HIDDEN, D_MODEL), l),
                  _resident((1, D_MODEL), l)],
        out_specs=row,
        out_shape=jax.ShapeDtypeStruct((T, D_MODEL), f32),
        compiler_params=_params("parallel"),
        name="ffn",
    )(x2, g_pre, wg, wu, wd, g_post)


def _layout_w_in(w):
    sizes = (MLA_Q_RANK, MLA_KV_RANK, MLA_ROPE, CONV_CH, CONV_CH, RET_HEADS * RET_DK, RET_HEADS * RET_DK,
             RET_HEADS * RET_DV, RET_HEADS * RET_DV, N_BRANCH * D_MODEL)
    pieces = []
    start = 0
    for n in sizes:
        pieces.append(w[:, :, start:start + n])
        start += n
    c_q, c_kv, k_pe, c_a, c_g, r_q, r_k, r_v, r_g, gates = pieces
    lead = w.shape[:2]
    kpe_blk = jnp.concatenate([jnp.zeros(lead + (MLA_NOPE,), w.dtype), k_pe,
                               jnp.zeros(lead + (LANES - MLA_NOPE - MLA_ROPE,), w.dtype)], axis=2)
    main = jnp.concatenate([gates, c_q, c_kv, kpe_blk, r_q, r_k, r_v, r_g], axis=2)
    conv = jnp.concatenate([c_a, c_g], axis=2)
    assert main.shape[2] == PROJ_COLS and conv.shape[2] == CONV_COLS
    return main.astype(bf16), conv.astype(bf16)


def _layout_heads(w):
    depth, rows, heads, width = w.shape
    w = jnp.pad(w, ((0, 0), (0, 0), (0, 0), (0, LANES - width)))
    return w.reshape(depth, rows, heads * LANES).astype(bf16)


def _rows(a):
    return a.reshape(a.shape[0], 1, a.shape[1])


def kernel(x, positions, ln_mix_pre, ln_mix_post, ln_ffn_pre, ln_ffn_post, w_in, mla_q_norm, mla_w_uq,
           mla_kv_norm, mla_w_ukv, mla_w_o, conv_w_dw, conv_b_dw, conv_ln_g, conv_ln_b, conv_w_pw,
           ret_decay_logits, ret_gn_g, ret_w_o, w_out, ffn_w_gate, ffn_w_up, ffn_w_down):
    B, S, _ = x.shape
    depth = w_in.shape[0]
    T = B * S
    cos_m, sin_m, cos_r, sin_r = _rope_tables(positions)

    w_main, w_conv = _layout_w_in(w_in)
    wq = _layout_heads(mla_w_uq.reshape(depth, MLA_Q_RANK, MLA_HEADS, MLA_NOPE + MLA_ROPE))
    w_ukv = mla_w_ukv.reshape(depth, MLA_KV_RANK, MLA_HEADS, MLA_NOPE + MLA_V)
    wk = _layout_heads(w_ukv[..., :MLA_NOPE])
    wv = _layout_heads(w_ukv[..., MLA_NOPE:])
    conv_w = (jnp.broadcast_to(conv_w_dw[:, :, None, :], (depth, CONV_WIDTH, SUBLANES, CONV_CH)),
              _rows(conv_b_dw), _rows(conv_ln_g), _rows(conv_ln_b))
    lg = jnp.broadcast_to(jnp.swapaxes(ret_decay_logits, 1, 2)[..., None], (depth, RET_HEADS, 2, RET_DV))
    w_mla, w_pw, w_ret, w_o = (a.astype(bf16) for a in (mla_w_o, conv_w_pw, ret_w_o, w_out))
    wg, wu, wd = (a.astype(bf16) for a in (ffn_w_gate, ffn_w_up, ffn_w_down))

    x2 = x.reshape(T, D_MODEL)
    for l in range(depth):
        h, proj_conv = _norm_conv_proj(x2, _rows(ln_mix_pre), w_conv, l)
        proj, o_conv = _inproj_conv(h, w_main, proj_conv, conv_w, l, B, S)
        q, k, v = _mla_prep(proj, cos_m, sin_m, _rows(mla_q_norm), _rows(mla_kv_norm), wq, wk, wv, l, B, S)
        o_mla = _attention(q, k, v, B, S).reshape(T, MLA_HEADS * MLA_V)
        o_ret = _ret_branch(proj, cos_r, sin_r, lg, _rows(ret_gn_g), l, B, S)
        x2 = _merge(x2, proj, o_mla, o_conv, o_ret, w_mla, w_pw, w_ret, w_o, _rows(ln_mix_post), l)
        x2 = _ffn(x2, _rows(ln_ffn_pre), wg, wu, wd, _rows(ln_ffn_post), l)
    return x2.reshape(B, S, D_MODEL)
```

```python
import functools

import jax
import jax.numpy as jnp
from jax import lax
from jax.experimental import pallas as pl
from jax.experimental.pallas import tpu as pltpu

D_MODEL = 1024
MLA_HEADS = 8
MLA_NOPE = 64
MLA_ROPE = 32
MLA_V = 64
MLA_Q_RANK = 256
MLA_KV_RANK = 128
CONV_CH = 512
CONV_WIDTH = 31
RET_HEADS = 4
RET_DK = 128
RET_DV = 256
RET_CHUNK = 128
FFN_HIDDEN = 2816
N_BRANCH = 3
ROPE_BASE = 10000.0
EPS = 1e-6
LOG2_E = 1.4426950408889634

LANES = 128
SUBLANES = 8
ATTN_HEADS_PER_STEP = 2
MERGE_ROWS = 512
HALO = 16

OFF_GATE = 0
OFF_CQ = OFF_GATE + N_BRANCH * D_MODEL
OFF_CKV = OFF_CQ + MLA_Q_RANK
OFF_RQ = OFF_CKV + 2 * LANES
OFF_RK = OFF_RQ + RET_HEADS * RET_DK
OFF_RV = OFF_RK + RET_HEADS * RET_DK
OFF_RG = OFF_RV + RET_HEADS * RET_DV
PROJ_COLS = OFF_RG + RET_HEADS * RET_DV
CONV_COLS = 2 * CONV_CH
PROJ_CHUNKS = (1792, 1792, 1536, 1536)
assert sum(PROJ_CHUNKS) == PROJ_COLS

VMEM_LIMIT = 56 * 1024 * 1024

f32 = jnp.float32
bf16 = jnp.bfloat16


def _params(*sem):
    return pltpu.CompilerParams(dimension_semantics=sem, vmem_limit_bytes=VMEM_LIMIT)


def _rms(x, g):
    return x * lax.rsqrt(jnp.mean(x * x, axis=-1, keepdims=True) + EPS) * g


def _sigmoid(x):
    return 1.0 / (1.0 + jnp.exp(-x))


def _rope_table_kernel(pos_ref, inv_ref, sign_ref, cm_ref, sm_ref, cr_ref, sr_ref):
    tm = pos_ref.shape[0]
    pos = pos_ref[...].astype(f32)

    hr = tm // 2
    half = RET_DK // 2
    low = lax.broadcasted_iota(jnp.int32, (hr, LANES), 1) < half
    ang = jnp.where(low, pos[0:hr], pos[hr:tm]) * inv_ref[1:2, :]
    c = jnp.cos(ang)
    s = jnp.sin(ang)
    c_sw = pltpu.roll(c, half, 1)
    s_sw = pltpu.roll(s, half, 1)
    sign_r = sign_ref[1:2, :]
    cr_ref[0:hr, :] = jnp.where(low, c, c_sw)
    cr_ref[hr:tm, :] = jnp.where(low, c_sw, c)
    sr_ref[0:hr, :] = jnp.where(low, s, s_sw) * sign_r
    sr_ref[hr:tm, :] = jnp.where(low, s_sw, s) * sign_r

    nf = MLA_ROPE // 2
    groups = LANES // nf
    gm = tm // groups
    lane = lax.broadcasted_iota(jnp.int32, (gm, LANES), 1)
    grp = lane // nf
    p = pos[(groups - 1) * gm:groups * gm]
    for g in range(groups - 2, -1, -1):
        p = jnp.where(grp == g, pos[g * gm:(g + 1) * gm], p)
    ang = p * inv_ref[0:1, :]
    c = jnp.cos(ang)
    s = jnp.sin(ang)
    first = (lane >= MLA_NOPE) & (lane < MLA_NOPE + nf)
    second = (lane >= MLA_NOPE + nf) & (lane < MLA_NOPE + MLA_ROPE)
    sign_m = sign_ref[0:1, :]

    def place(x, g, fill):
        def shifted(target):
            shift = (target - nf * g) % LANES
            return pltpu.roll(x, shift, 1) if shift else x
        return jnp.where(first, shifted(MLA_NOPE), jnp.where(second, shifted(MLA_NOPE + nf), fill))

    for g in range(groups):
        rows = slice(g * gm, (g + 1) * gm)
        cm_ref[rows, :] = place(c, g, 1.0)
        sm_ref[rows, :] = place(s, g, 0.0) * sign_m


def _rope_tables(positions):
    T = positions.size
    tm = min(2048, T)
    inv_m = ROPE_BASE ** (-jnp.arange(0, MLA_ROPE, 2, dtype=f32) / MLA_ROPE)
    inv_r = ROPE_BASE ** (-jnp.arange(0, RET_DK, 2, dtype=f32) / RET_DK)
    half = MLA_ROPE // 2
    inv = jnp.stack([jnp.tile(inv_m, LANES // half), jnp.concatenate([inv_r, inv_r])])
    sign = jnp.stack([
        jnp.concatenate([jnp.ones((MLA_NOPE,), f32), -jnp.ones((half,), f32),
                         jnp.ones((LANES - MLA_NOPE - half,), f32)]),
        jnp.concatenate([-jnp.ones((RET_DK // 2,), f32), jnp.ones((RET_DK // 2,), f32)])])
    tab = jax.ShapeDtypeStruct((T, LANES), f32)
    small = pl.BlockSpec((2, LANES), lambda i: (0, 0))
    row = pl.BlockSpec((tm, LANES), lambda i: (i, 0))
    return pl.pallas_call(
        _rope_table_kernel,
        grid=(T // tm,),
        in_specs=[pl.BlockSpec((tm, 1), lambda i: (i, 0)), small, small],
        out_specs=[row, row, row, row],
        out_shape=[tab, tab, tab, tab],
        compiler_params=_params("parallel"),
        name="rope_tables",
    )(positions.reshape(T, 1), inv, sign)


def _resident(shape, l):
    return pl.BlockSpec((None,) + shape, lambda *_: (l,) + (0,) * len(shape), pipeline_mode=pl.Buffered(1))


def _norm_conv_proj_kernel(x_ref, g_ref, w_ref, h_ref, pc_ref):
    h = _rms(x_ref[...], g_ref[...]).astype(bf16)
    h_ref[...] = h
    pc_ref[...] = jnp.dot(h, w_ref[...], preferred_element_type=f32).astype(pc_ref.dtype)


def _norm_conv_proj(x2, g, w_conv, l):
    T = x2.shape[0]
    tm = min(1024, T)
    return pl.pallas_call(
        _norm_conv_proj_kernel,
        grid=(T // tm,),
        in_specs=[pl.BlockSpec((tm, D_MODEL), lambda i: (i, 0)), _resident((1, D_MODEL), l),
                  _resident((D_MODEL, CONV_COLS), l)],
        out_specs=[pl.BlockSpec((tm, D_MODEL), lambda i: (i, 0)),
                   pl.BlockSpec((tm, CONV_COLS), lambda i: (i, 0))],
        out_shape=[jax.ShapeDtypeStruct((T, D_MODEL), bf16), jax.ShapeDtypeStruct((T, CONV_COLS), bf16)],
        compiler_params=_params("parallel"),
        name="norm_conv_proj",
    )(x2, g, w_conv)


def _inproj_conv_kernel(h_ref, w_ref, a_ref, g_ref, ap_ref, gp_ref, an_ref, gn_ref,
                        wdw_ref, bdw_ref, lng_ref, lnb_ref, o_ref, oc_ref, ext_ref, sh_ref):
    ts = h_ref.shape[0]
    _conv_prepare(a_ref, g_ref, ap_ref, gp_ref, an_ref, gn_ref, ext_ref, sh_ref,
                  pl.program_id(1), pl.num_programs(1) - 1)
    h = h_ref[...]
    rows = min(CONV_ROWS, ts)
    n_conv = ts // rows
    col = 0
    done = 0
    for c, width in enumerate(PROJ_CHUNKS):
        o_ref[:, col:col + width] = jnp.dot(h, w_ref[:, col:col + width],
                                            preferred_element_type=f32).astype(o_ref.dtype)
        col += width
        upto = (c + 1) * n_conv // len(PROJ_CHUNKS)
        for k in range(done, upto):
            _conv_rows(k * rows, rows, wdw_ref, bdw_ref, lng_ref, lnb_ref, oc_ref, ext_ref, sh_ref)
        done = upto


def _inproj_conv(h, w, pc, conv_w, l, B, S):
    ts = min(512, S)
    nS = S // ts
    w_dw, b_dw, ln_g, ln_b = conv_w
    row = lambda n: pl.BlockSpec((ts, n), lambda b, i: (b * nS + i, 0))
    return pl.pallas_call(
        _inproj_conv_kernel,
        grid=(B, nS),
        in_specs=[row(D_MODEL), _resident((D_MODEL, PROJ_COLS), l)] + _conv_specs(ts, nS, B * S, l),
        out_specs=[row(PROJ_COLS), row(CONV_CH)],
        out_shape=[jax.ShapeDtypeStruct((B * S, PROJ_COLS), bf16),
                   jax.ShapeDtypeStruct((B * S, CONV_CH), bf16)],
        scratch_shapes=[pltpu.VMEM((ts + 2 * HALO, CONV_CH), f32),
                        pltpu.VMEM((SUBLANES - 1, ts + 2 * HALO - SUBLANES, CONV_CH), f32)],
        compiler_params=_params("parallel", "parallel"),
        name="in_proj_conv",
    )(h, w, pc, pc, pc, pc, pc, pc, w_dw, b_dw, ln_g, ln_b)


def _swap_halves_mla(x, lane):
    return jnp.where(lane < MLA_NOPE + MLA_ROPE // 2,
                     pltpu.roll(x, LANES - MLA_ROPE // 2, 1), pltpu.roll(x, MLA_ROPE // 2, 1))


def _mla_prep_kernel(cq_ref, ckv_ref, cos_ref, sin_ref, qn_ref, kvn_ref, wq_ref, wqs_ref, wk_ref, wv_ref,
                     q_out, k_out, v_out):
    tm = cq_ref.shape[0]
    lane = lax.broadcasted_iota(jnp.int32, (tm, LANES), 1)
    cos = cos_ref[...]
    sin = sin_ref[...]
    scale = (MLA_NOPE + MLA_ROPE) ** -0.5 * LOG2_E
    cos_q = cos * scale
    sin_q = sin * scale

    hq = _rms(cq_ref[...].astype(f32), qn_ref[...]).astype(bf16)
    q = jnp.dot(hq, wq_ref[...], preferred_element_type=f32)
    q_sw = jnp.dot(hq, wqs_ref[...], preferred_element_type=f32)
    for h in range(MLA_HEADS):
        cols = slice(h * LANES, (h + 1) * LANES)
        q_out[0, h] = (q[:, cols] * cos_q + q_sw[:, cols] * sin_q).astype(bf16)

    blk = ckv_ref[...].astype(f32)
    hkv = _rms(blk[:, :MLA_KV_RANK], kvn_ref[...]).astype(bf16)
    kpe = blk[:, MLA_KV_RANK:]
    kpe = kpe * cos + _swap_halves_mla(kpe, lane) * sin
    kk = jnp.dot(hkv, wk_ref[...], preferred_element_type=f32)
    vv = jnp.dot(hkv, wv_ref[...], preferred_element_type=f32)
    ones_col = (lane == MLA_V).astype(f32)
    for h in range(MLA_HEADS):
        k_out[0, h] = (kk[:, h * LANES:(h + 1) * LANES] + kpe).astype(bf16)
        v_out[0, h] = (vv[:, h * LANES:(h + 1) * LANES] + ones_col).astype(bf16)


def _mla_prep(proj, cos_m, sin_m, qn, kvn, wq, wq_sw, wk, wv, l, B, S):
    tm = min(512, S)
    nS = S // tm
    hm = jax.ShapeDtypeStruct((B, MLA_HEADS, S, LANES), bf16)
    hm_spec = pl.BlockSpec((1, MLA_HEADS, tm, LANES), lambda b, i: (b, 0, i, 0))
    full = lambda shape: _resident(shape, l)
    return pl.pallas_call(
        _mla_prep_kernel,
        grid=(B, nS),
        in_specs=[pl.BlockSpec((tm, MLA_Q_RANK), lambda b, i: (b * nS + i, OFF_CQ // MLA_Q_RANK)),
                  pl.BlockSpec((tm, 2 * LANES), lambda b, i: (b * nS + i, OFF_CKV // (2 * LANES))),
                  pl.BlockSpec((tm, LANES), lambda b, i: (b * nS + i, 0)),
                  pl.BlockSpec((tm, LANES), lambda b, i: (b * nS + i, 0)),
                  full((1, MLA_Q_RANK)), full((1, MLA_KV_RANK)),
                  full((MLA_Q_RANK, MLA_HEADS * LANES)), full((MLA_Q_RANK, MLA_HEADS * LANES)),
                  full((MLA_KV_RANK, MLA_HEADS * LANES)),
                  full((MLA_KV_RANK, MLA_HEADS * LANES))],
        out_specs=[hm_spec, hm_spec, hm_spec],
        out_shape=[hm, hm, hm],
        compiler_params=_params("parallel", "parallel"),
        name="mla_prep",
    )(proj, proj, cos_m, sin_m, qn, kvn, wq, wq_sw, wk, wv)


def _attn_kernel(q_ref, k_ref, v_ref, o_ref, *, tk):
    tq = q_ref.shape[2]
    S = k_ref.shape[2]
    heads = q_ref.shape[1]
    qs = [q_ref[0, hh] for hh in range(heads)]

    def body(j, carry):
        start = pl.multiple_of(j * tk, tk)
        new = []
        for hh in range(heads):
            m, acc = carry[hh]
            ks = k_ref[0, hh, pl.ds(start, tk), :]
            vs = v_ref[0, hh, pl.ds(start, tk), :]
            s = lax.dot_general(qs[hh], ks, (((1,), (1,)), ((), ())), preferred_element_type=f32)
            m_new = jnp.maximum(m, jnp.max(s, axis=-1, keepdims=True))
            p = jnp.exp2(s - m_new)
            acc = acc * jnp.exp2(m - m_new) + jnp.dot(p.astype(bf16), vs, preferred_element_type=f32)
            new.append((m_new, acc))
        return tuple(new)

    init = tuple((jnp.full((tq, 1), -1e30, f32), jnp.zeros((tq, LANES), f32)) for _ in range(heads))
    res = lax.fori_loop(0, S // tk, body, init, unroll=True)
    outs = [acc / acc[:, MLA_V:MLA_V + 1] for _, acc in res]
    lane = lax.broadcasted_iota(jnp.int32, (tq, LANES), 1)
    for p in range(heads // 2):
        pair = jnp.where(lane < MLA_V, outs[2 * p], pltpu.roll(outs[2 * p + 1], MLA_V, 1))
        o_ref[0, :, p * LANES:(p + 1) * LANES] = pair.astype(bf16)


def _attention(q, k, v, B, S):
    tq = min(1024, S)
    tk = min(2048, S)
    hs = ATTN_HEADS_PER_STEP
    return pl.pallas_call(
        functools.partial(_attn_kernel, tk=tk),
        grid=(B, MLA_HEADS // hs, S // tq),
        in_specs=[pl.BlockSpec((1, hs, tq, LANES), lambda b, h, i: (b, h, i, 0)),
                  pl.BlockSpec((1, hs, S, LANES), lambda b, h, i: (b, h, 0, 0)),
                  pl.BlockSpec((1, hs, S, LANES), lambda b, h, i: (b, h, 0, 0))],
        out_specs=pl.BlockSpec((1, tq, hs * MLA_V), lambda b, h, i: (b, i, h)),
        out_shape=jax.ShapeDtypeStruct((B, S, MLA_HEADS * MLA_V), bf16),
        compiler_params=_params("parallel", "parallel", "arbitrary"),
        name="mla_attention",
    )(q, k, v)


CONV_ROWS = 64


def _conv_prepare(a_ref, g_ref, ap_ref, gp_ref, an_ref, gn_ref, ext_ref, sh_ref, i, last):
    ts = a_ref.shape[0]

    def glu(a, g):
        return a[...].astype(f32) * _sigmoid(g[...].astype(f32))

    ext_ref[HALO:HALO + ts, :] = glu(a_ref, g_ref)
    prev = glu(ap_ref, gp_ref)
    ext_ref[0:HALO, :] = jnp.where(i > 0, prev, 0.0)
    nxt = glu(an_ref, gn_ref)
    ext_ref[HALO + ts:2 * HALO + ts, :] = jnp.where(i < last, nxt, 0.0)
    n_sh = sh_ref.shape[1]
    for r in range(1, SUBLANES):
        sh_ref[r - 1] = ext_ref[r:r + n_sh, :]


def _conv_rows(r0, rows, w_ref, b_ref, lng_ref, lnb_ref, o_ref, ext_ref, sh_ref):
    first = HALO - CONV_WIDTH // 2
    acc = jnp.broadcast_to(b_ref[...], (rows // SUBLANES, SUBLANES, CONV_CH))
    for j in range(CONV_WIDTH):
        r = (first + j) % SUBLANES
        base = pl.ds(r0 + (first + j - r), rows)
        tap = ext_ref[base, :] if r == 0 else sh_ref[r - 1, base, :]
        acc = acc + tap.reshape(rows // SUBLANES, SUBLANES, CONV_CH) * w_ref[j]
    acc = acc.reshape(rows, CONV_CH)
    mu = jnp.mean(acc, axis=-1, keepdims=True)
    d = acc - mu
    var = jnp.mean(d * d, axis=-1, keepdims=True)
    y = d * lax.rsqrt(var + EPS) * lng_ref[...] + lnb_ref[...]
    o_ref[pl.ds(r0, rows), :] = (y * _sigmoid(y)).astype(bf16)


def _conv_specs(ts, nS, n_rows, l):
    hb = ts // HALO
    nhb = n_rows // HALO
    ca, cg = 0, 1

    def main(col):
        return pl.BlockSpec((ts, CONV_CH), lambda b, i: (b * nS + i, col))

    def prev(col):
        return pl.BlockSpec((HALO, CONV_CH), lambda b, i: (jnp.maximum((b * nS + i) * hb - 1, 0), col))

    def nxt(col):
        return pl.BlockSpec((HALO, CONV_CH), lambda b, i: (jnp.minimum((b * nS + i + 1) * hb, nhb - 1), col))

    vec = _resident((1, CONV_CH), l)
    return [main(ca), main(cg), prev(ca), prev(cg), nxt(ca), nxt(cg),
            _resident((CONV_WIDTH, SUBLANES, CONV_CH), l), vec, vec, vec]


def _ret_kernel(q_ref, k_ref, v_ref, g_ref, cos_ref, sin_ref, lg_ref, gn_ref, o_ref,
                kb, sall, ubuf):
    C = RET_CHUNK
    S = q_ref.shape[0]
    NC = S // C
    lgt = lg_ref[0]
    ls = jnp.minimum(lgt, 0.0) - jnp.log(1.0 + jnp.exp(-jnp.abs(lgt)))
    lgf = ls[0:1, :LANES]
    lgb = ls[1:2, :LANES]
    ri = lax.broadcasted_iota(jnp.int32, (C, LANES), 0).astype(f32)
    ci = lax.broadcasted_iota(jnp.int32, (C, LANES), 1).astype(f32)
    diff = ri - ci
    dmat = jnp.where(diff >= 0, jnp.exp(jnp.maximum(diff, 0.0) * lgf), jnp.exp(jnp.maximum(-diff, 0.0) * lgb))
    qf_dec = jnp.exp((ri + 1.0) * lgf)
    qb_dec = jnp.exp((C - ri) * lgb)
    kf_dec = jnp.exp((C - 1.0 - ri) * lgf)
    kb_dec = jnp.exp(ri * lgb)
    cdf = jnp.exp(C * ls[0:1, :])
    cdb = jnp.exp(C * ls[1:2, :])
    kscale = RET_DK ** -0.5
    contract0 = (((0,), (0,)), ((), ()))

    def chunk_rows(c):
        return pl.ds(pl.multiple_of(c * C, C), C)

    def rope(x, rows):
        x = x.astype(f32)
        return x * cos_ref[rows, :] + pltpu.roll(x, RET_DK // 2, 1) * sin_ref[rows, :]

    def increments(c, _):
        rows = chunk_rows(c)
        kr = rope(k_ref[rows, :], rows) * kscale
        kb[rows, :] = kr.astype(bf16)
        kcat = jnp.concatenate([(kr * kf_dec).astype(bf16), (kr * kb_dec).astype(bf16)], axis=1)
        ubuf[c] = lax.dot_general(kcat, v_ref[rows, :], contract0, preferred_element_type=f32)
        return 0

    lax.fori_loop(0, NC, increments, 0, unroll=8)

    def scan_fwd(c, sf):
        sall[c, 0:C, :] = sf.astype(bf16)
        return cdf * sf + ubuf[c, 0:C, :]

    def scan_bwd(t, sb):
        c = NC - 1 - t
        sall[c, C:2 * C, :] = sb.astype(bf16)
        return cdb * sb + ubuf[c, C:2 * C, :]

    zero_state = jnp.zeros((RET_DK, RET_DV), f32)
    lax.fori_loop(0, NC, scan_fwd, zero_state, unroll=2)
    lax.fori_loop(0, NC, scan_bwd, zero_state, unroll=2)

    gn = gn_ref[...]

    def outputs(c, _):
        rows = chunk_rows(c)
        qr = rope(q_ref[rows, :], rows)
        s = lax.dot_general(qr.astype(bf16), kb[rows, :], (((1,), (1,)), ((), ())),
                            preferred_element_type=f32) * dmat
        o = jnp.dot(s.astype(bf16), v_ref[rows, :], preferred_element_type=f32)
        qcat = jnp.concatenate([(qr * qf_dec).astype(bf16), (qr * qb_dec).astype(bf16)], axis=1)
        o = o + jnp.dot(qcat, sall[c], preferred_element_type=f32)
        mu = jnp.mean(o, axis=-1, keepdims=True)
        d = o - mu
        var = jnp.mean(d * d, axis=-1, keepdims=True)
        on = d * lax.rsqrt(var + EPS) * gn
        g = g_ref[rows, :].astype(f32)
        o_ref[rows, :] = (g * _sigmoid(g) * on).astype(bf16)
        return 0

    lax.fori_loop(0, NC, outputs, 0, unroll=8)


def _ret_branch(proj, cos_r, sin_r, lg, gn_g, l, B, S):
    nq = OFF_RQ // RET_DK
    nk = OFF_RK // RET_DK
    nv = OFF_RV // RET_DV
    ng = OFF_RG // RET_DV
    return pl.pallas_call(
        _ret_kernel,
        grid=(B, RET_HEADS),
        in_specs=[pl.BlockSpec((S, RET_DK), lambda b, h: (b, nq + h)),
                  pl.BlockSpec((S, RET_DK), lambda b, h: (b, nk + h)),
                  pl.BlockSpec((S, RET_DV), lambda b, h: (b, nv + h)),
                  pl.BlockSpec((S, RET_DV), lambda b, h: (b, ng + h)),
                  pl.BlockSpec((S, LANES), lambda b, h: (b, 0)),
                  pl.BlockSpec((S, LANES), lambda b, h: (b, 0)),
                  pl.BlockSpec((None, 1, 2, RET_DV), lambda b, h: (l, h, 0, 0)),
                  pl.BlockSpec((None, 1, RET_DV), lambda b, h: (l, 0, h))],
        out_specs=pl.BlockSpec((S, RET_DV), lambda b, h: (b, h)),
        out_shape=jax.ShapeDtypeStruct((B * S, RET_HEADS * RET_DV), bf16),
        scratch_shapes=[pltpu.VMEM((S, RET_DK), bf16),
                        pltpu.VMEM((S // RET_CHUNK, 2 * RET_DK, RET_DV), bf16),
                        pltpu.VMEM((S // RET_CHUNK, 2 * RET_DK, RET_DV), f32)],
        compiler_params=_params("parallel", "parallel"),
        name="retention",
    )(proj, proj, proj, proj, cos_r, sin_r, lg, gn_g)


def _merge_kernel(x_ref, g0_ref, g1_ref, g2_ref, om_ref, oc_ref, or_ref,
                  wm_ref, wc_ref, wr_ref, wo_ref, ln_ref, o_ref):
    tm = x_ref.shape[0]
    rb = min(MERGE_ROWS, tm)
    for r in range(tm // rb):
        rows = slice(r * rb, (r + 1) * rb)
        gate = lambda ref: _sigmoid(ref[rows, :].astype(f32))
        merged = gate(g0_ref) * jnp.dot(om_ref[rows, :], wm_ref[...], preferred_element_type=f32)
        merged = merged + gate(g1_ref) * jnp.dot(oc_ref[rows, :], wc_ref[...], preferred_element_type=f32)
        merged = merged + gate(g2_ref) * jnp.dot(or_ref[rows, :], wr_ref[...], preferred_element_type=f32)
        z = jnp.dot(merged.astype(bf16), wo_ref[...], preferred_element_type=f32)
        o_ref[rows, :] = x_ref[rows, :] + _rms(z, ln_ref[...])


def _merge(x2, proj, o_mla, o_conv, o_ret, w_mla, w_pw, w_ret, w_out, ln_post, l):
    T = x2.shape[0]
    tm = min(2 * MERGE_ROWS, T)
    row = lambda n, col=0: pl.BlockSpec((tm, n), lambda i: (i, col))
    return pl.pallas_call(
        _merge_kernel,
        grid=(T // tm,),
        in_specs=[row(D_MODEL), row(D_MODEL, 0), row(D_MODEL, 1), row(D_MODEL, 2),
                  row(MLA_HEADS * MLA_V), row(CONV_CH), row(RET_HEADS * RET_DV),
                  _resident((MLA_HEADS * MLA_V, D_MODEL), l), _resident((CONV_CH, D_MODEL), l),
                  _resident((RET_HEADS * RET_DV, D_MODEL), l), _resident((D_MODEL, D_MODEL), l),
                  _resident((1, D_MODEL), l)],
        out_specs=row(D_MODEL),
        out_shape=jax.ShapeDtypeStruct((T, D_MODEL), f32),
        compiler_params=_params("parallel"),
        name="merge_out",
    )(x2, proj, proj, proj, o_mla, o_conv, o_ret, w_mla, w_pw, w_ret, w_out, ln_post)


def _ffn_kernel(x_ref, gpre_ref, wg_ref, wu_ref, wd_ref, gpost_ref, o_ref):
    x = x_ref[...]
    h = _rms(x, gpre_ref[...]).astype(bf16)
    a = jnp.dot(h, wg_ref[...], preferred_element_type=f32)
    u = jnp.dot(h, wu_ref[...], preferred_element_type=f32)
    act = (a * _sigmoid(a) * u).astype(bf16)
    f = jnp.dot(act, wd_ref[...], preferred_element_type=f32)
    o_ref[...] = x + _rms(f, gpost_ref[...])


def _ffn(x2, g_pre, wg, wu, wd, g_post, l):
    T = x2.shape[0]
    tm = min(512, T)
    row = pl.BlockSpec((tm, D_MODEL), lambda i: (i, 0))
    return pl.pallas_call(
        _ffn_kernel,
        grid=(T // tm,),
        in_specs=[row, _resident((1, D_MODEL), l), _resident((D_MODEL, FFN_HIDDEN), l),
                  _resident((D_MODEL, FFN_HIDDEN), l), _resident((FFN_HIDDEN, D_MODEL), l),
                  _resident((1, D_MODEL), l)],
        out_specs=row,
        out_shape=jax.ShapeDtypeStruct((T, D_MODEL), f32),
        compiler_params=_params("parallel"),
        name="ffn",
    )(x2, g_pre, wg, wu, wd, g_post)


def _layout_w_in(w):
    sizes = (MLA_Q_RANK, MLA_KV_RANK, MLA_ROPE, CONV_CH, CONV_CH, RET_HEADS * RET_DK, RET_HEADS * RET_DK,
             RET_HEADS * RET_DV, RET_HEADS * RET_DV, N_BRANCH * D_MODEL)
    pieces = []
    start = 0
    for n in sizes:
        pieces.append(w[:, :, start:start + n])
        start += n
    c_q, c_kv, k_pe, c_a, c_g, r_q, r_k, r_v, r_g, gates = pieces
    lead = w.shape[:2]
    kpe_blk = jnp.concatenate([jnp.zeros(lead + (MLA_NOPE,), w.dtype), k_pe,
                               jnp.zeros(lead + (LANES - MLA_NOPE - MLA_ROPE,), w.dtype)], axis=2)
    main = jnp.concatenate([gates, c_q, c_kv, kpe_blk, r_q, r_k, r_v, r_g], axis=2)
    conv = jnp.concatenate([c_a, c_g], axis=2)
    assert main.shape[2] == PROJ_COLS and conv.shape[2] == CONV_COLS
    return main.astype(bf16), conv.astype(bf16)


def _layout_heads(w):
    depth, rows, heads, width = w.shape
    w = jnp.pad(w, ((0, 0), (0, 0), (0, 0), (0, LANES - width)))
    return w.reshape(depth, rows, heads * LANES).astype(bf16)


def _rows(a):
    return a.reshape(a.shape[0], 1, a.shape[1])


def kernel(x, positions, ln_mix_pre, ln_mix_post, ln_ffn_pre, ln_ffn_post, w_in, mla_q_norm, mla_w_uq,
           mla_kv_norm, mla_w_ukv, mla_w_o, conv_w_dw, conv_b_dw, conv_ln_g, conv_ln_b, conv_w_pw,
           ret_decay_logits, ret_gn_g, ret_w_o, w_out, ffn_w_gate, ffn_w_up, ffn_w_down):
    B, S, _ = x.shape
    depth = w_in.shape[0]
    T = B * S
    cos_m, sin_m, cos_r, sin_r = _rope_tables(positions)

    w_main, w_conv = _layout_w_in(w_in)
    w_uq = mla_w_uq.reshape(depth, MLA_Q_RANK, MLA_HEADS, MLA_NOPE + MLA_ROPE)
    wq = _layout_heads(w_uq)
    mid = MLA_NOPE + MLA_ROPE // 2
    wq_sw = _layout_heads(jnp.concatenate([w_uq[..., :MLA_NOPE], w_uq[..., mid:], w_uq[..., MLA_NOPE:mid]], -1))
    w_ukv = mla_w_ukv.reshape(depth, MLA_KV_RANK, MLA_HEADS, MLA_NOPE + MLA_V)
    wk = _layout_heads(w_ukv[..., :MLA_NOPE])
    wv = _layout_heads(w_ukv[..., MLA_NOPE:])
    conv_w = (jnp.broadcast_to(conv_w_dw[:, :, None, :], (depth, CONV_WIDTH, SUBLANES, CONV_CH)),
              _rows(conv_b_dw), _rows(conv_ln_g), _rows(conv_ln_b))
    lg = jnp.broadcast_to(jnp.swapaxes(ret_decay_logits, 1, 2)[..., None], (depth, RET_HEADS, 2, RET_DV))
    w_mla, w_pw, w_ret, w_o = (a.astype(bf16) for a in (mla_w_o, conv_w_pw, ret_w_o, w_out))
    wg, wu, wd = (a.astype(bf16) for a in (ffn_w_gate, ffn_w_up, ffn_w_down))

    x2 = x.reshape(T, D_MODEL)
    for l in range(depth):
        h, proj_conv = _norm_conv_proj(x2, _rows(ln_mix_pre), w_conv, l)
        proj, o_conv = _inproj_conv(h, w_main, proj_conv, conv_w, l, B, S)
        q, k, v = _mla_prep(proj, cos_m, sin_m, _rows(mla_q_norm), _rows(mla_kv_norm), wq, wq_sw, wk, wv, l, B, S)
        o_mla = _attention(q, k, v, B, S).reshape(T, MLA_HEADS * MLA_V)
        o_ret = _ret_branch(proj, cos_r, sin_r, lg, _rows(ret_gn_g), l, B, S)
        x2 = _merge(x2, proj, o_mla, o_conv, o_ret, w_mla, w_pw, w_ret, w_o, _rows(ln_mix_post), l)
        x2 = _ffn(x2, _rows(ln_ffn_pre), wg, wu, wd, _rows(ln_ffn_post), l)
    return x2.reshape(B, S, D_MODEL)
```

```python
import functools

import jax
import jax.numpy as jnp
from jax import lax
from jax.experimental import pallas as pl
from jax.experimental.pallas import tpu as pltpu

D_MODEL = 1024
MLA_HEADS = 8
MLA_NOPE = 64
MLA_ROPE = 32
MLA_V = 64
MLA_Q_RANK = 256
MLA_KV_RANK = 128
CONV_CH = 512
CONV_WIDTH = 31
RET_HEADS = 4
RET_DK = 128
RET_DV = 256
RET_CHUNK = 128
FFN_HIDDEN = 2816
N_BRANCH = 3
ROPE_BASE = 10000.0
EPS = 1e-6
LOG2_E = 1.4426950408889634

LANES = 128
SUBLANES = 8
ATTN_HEADS_PER_STEP = 4
MERGE_ROWS = 512
HALO = 16

OFF_GATE = 0
OFF_CQ = OFF_GATE + N_BRANCH * D_MODEL
OFF_CKV = OFF_CQ + MLA_Q_RANK
OFF_RQ = OFF_CKV + 2 * LANES
OFF_RK = OFF_RQ + RET_HEADS * RET_DK
OFF_RV = OFF_RK + RET_HEADS * RET_DK
OFF_RG = OFF_RV + RET_HEADS * RET_DV
PROJ_COLS = OFF_RG + RET_HEADS * RET_DV
CONV_COLS = 2 * CONV_CH
PROJ_CHUNKS = (1792, 1792, 1536, 1536)
assert sum(PROJ_CHUNKS) == PROJ_COLS

VMEM_LIMIT = 56 * 1024 * 1024

f32 = jnp.float32
bf16 = jnp.bfloat16


def _params(*sem):
    return pltpu.CompilerParams(dimension_semantics=sem, vmem_limit_bytes=VMEM_LIMIT)


def _rms(x, g):
    return x * lax.rsqrt(jnp.mean(x * x, axis=-1, keepdims=True) + EPS) * g


def _sigmoid(x):
    return 1.0 / (1.0 + jnp.exp(-x))


def _rope_table_kernel(pos_ref, inv_ref, sign_ref, cm_ref, sm_ref, cr_ref, sr_ref):
    tm = pos_ref.shape[0]
    pos = pos_ref[...].astype(f32)

    hr = tm // 2
    half = RET_DK // 2
    low = lax.broadcasted_iota(jnp.int32, (hr, LANES), 1) < half
    ang = jnp.where(low, pos[0:hr], pos[hr:tm]) * inv_ref[1:2, :]
    c = jnp.cos(ang)
    s = jnp.sin(ang)
    c_sw = pltpu.roll(c, half, 1)
    s_sw = pltpu.roll(s, half, 1)
    sign_r = sign_ref[1:2, :]
    cr_ref[0:hr, :] = jnp.where(low, c, c_sw)
    cr_ref[hr:tm, :] = jnp.where(low, c_sw, c)
    sr_ref[0:hr, :] = jnp.where(low, s, s_sw) * sign_r
    sr_ref[hr:tm, :] = jnp.where(low, s_sw, s) * sign_r

    nf = MLA_ROPE // 2
    groups = LANES // nf
    gm = tm // groups
    lane = lax.broadcasted_iota(jnp.int32, (gm, LANES), 1)
    grp = lane // nf
    p = pos[(groups - 1) * gm:groups * gm]
    for g in range(groups - 2, -1, -1):
        p = jnp.where(grp == g, pos[g * gm:(g + 1) * gm], p)
    ang = p * inv_ref[0:1, :]
    c = jnp.cos(ang)
    s = jnp.sin(ang)
    first = (lane >= MLA_NOPE) & (lane < MLA_NOPE + nf)
    second = (lane >= MLA_NOPE + nf) & (lane < MLA_NOPE + MLA_ROPE)
    sign_m = sign_ref[0:1, :]

    def place(x, g, fill):
        def shifted(target):
            shift = (target - nf * g) % LANES
            return pltpu.roll(x, shift, 1) if shift else x
        return jnp.where(first, shifted(MLA_NOPE), jnp.where(second, shifted(MLA_NOPE + nf), fill))

    for g in range(groups):
        rows = slice(g * gm, (g + 1) * gm)
        cm_ref[rows, :] = place(c, g, 1.0)
        sm_ref[rows, :] = place(s, g, 0.0) * sign_m


def _rope_tables(positions):
    T = positions.size
    tm = min(2048, T)
    inv_m = ROPE_BASE ** (-jnp.arange(0, MLA_ROPE, 2, dtype=f32) / MLA_ROPE)
    inv_r = ROPE_BASE ** (-jnp.arange(0, RET_DK, 2, dtype=f32) / RET_DK)
    half = MLA_ROPE // 2
    inv = jnp.stack([jnp.tile(inv_m, LANES // half), jnp.concatenate([inv_r, inv_r])])
    sign = jnp.stack([
        jnp.concatenate([jnp.ones((MLA_NOPE,), f32), -jnp.ones((half,), f32),
                         jnp.ones((LANES - MLA_NOPE - half,), f32)]),
        jnp.concatenate([-jnp.ones((RET_DK // 2,), f32), jnp.ones((RET_DK // 2,), f32)])])
    tab = jax.ShapeDtypeStruct((T, LANES), f32)
    small = pl.BlockSpec((2, LANES), lambda i: (0, 0))
    row = pl.BlockSpec((tm, LANES), lambda i: (i, 0))
    return pl.pallas_call(
        _rope_table_kernel,
        grid=(T // tm,),
        in_specs=[pl.BlockSpec((tm, 1), lambda i: (i, 0)), small, small],
        out_specs=[row, row, row, row],
        out_shape=[tab, tab, tab, tab],
        compiler_params=_params("parallel"),
        name="rope_tables",
    )(positions.reshape(T, 1), inv, sign)


def _resident(shape, l):
    return pl.BlockSpec((None,) + shape, lambda *_: (l,) + (0,) * len(shape), pipeline_mode=pl.Buffered(1))


def _norm_conv_proj_kernel(x_ref, g_ref, w_ref, h_ref, pc_ref):
    h = _rms(x_ref[...], g_ref[...]).astype(bf16)
    h_ref[...] = h
    pc_ref[...] = jnp.dot(h, w_ref[...], preferred_element_type=f32).astype(pc_ref.dtype)


def _norm_conv_proj(x2, g, w_conv, l):
    T = x2.shape[0]
    tm = min(1024, T)
    return pl.pallas_call(
        _norm_conv_proj_kernel,
        grid=(T // tm,),
        in_specs=[pl.BlockSpec((tm, D_MODEL), lambda i: (i, 0)), _resident((1, D_MODEL), l),
                  _resident((D_MODEL, CONV_COLS), l)],
        out_specs=[pl.BlockSpec((tm, D_MODEL), lambda i: (i, 0)),
                   pl.BlockSpec((tm, CONV_COLS), lambda i: (i, 0))],
        out_shape=[jax.ShapeDtypeStruct((T, D_MODEL), bf16), jax.ShapeDtypeStruct((T, CONV_COLS), bf16)],
        compiler_params=_params("parallel"),
        name="norm_conv_proj",
    )(x2, g, w_conv)


def _inproj_conv_kernel(h_ref, w_ref, a_ref, g_ref, ap_ref, gp_ref, an_ref, gn_ref,
                        wdw_ref, bdw_ref, lng_ref, lnb_ref, o_ref, oc_ref, ext_ref, sh_ref):
    ts = h_ref.shape[0]
    _conv_prepare(a_ref, g_ref, ap_ref, gp_ref, an_ref, gn_ref, ext_ref, sh_ref,
                  pl.program_id(1), pl.num_programs(1) - 1)
    h = h_ref[...]
    rows = min(CONV_ROWS, ts)
    n_conv = ts // rows
    col = 0
    done = 0
    for c, width in enumerate(PROJ_CHUNKS):
        o_ref[:, col:col + width] = jnp.dot(h, w_ref[:, col:col + width],
                                            preferred_element_type=f32).astype(o_ref.dtype)
        col += width
        upto = (c + 1) * n_conv // len(PROJ_CHUNKS)
        for k in range(done, upto):
            _conv_rows(k * rows, rows, wdw_ref, bdw_ref, lng_ref, lnb_ref, oc_ref, ext_ref, sh_ref)
        done = upto


def _inproj_conv(h, w, pc, conv_w, l, B, S):
    ts = min(512, S)
    nS = S // ts
    w_dw, b_dw, ln_g, ln_b = conv_w
    row = lambda n: pl.BlockSpec((ts, n), lambda b, i: (b * nS + i, 0))
    return pl.pallas_call(
        _inproj_conv_kernel,
        grid=(B, nS),
        in_specs=[row(D_MODEL), _resident((D_MODEL, PROJ_COLS), l)] + _conv_specs(ts, nS, B * S, l),
        out_specs=[row(PROJ_COLS), row(CONV_CH)],
        out_shape=[jax.ShapeDtypeStruct((B * S, PROJ_COLS), bf16),
                   jax.ShapeDtypeStruct((B * S, CONV_CH), bf16)],
        scratch_shapes=[pltpu.VMEM((ts + 2 * HALO, CONV_CH), f32),
                        pltpu.VMEM((SUBLANES - 1, ts + 2 * HALO - SUBLANES, CONV_CH), f32)],
        compiler_params=_params("parallel", "parallel"),
        name="in_proj_conv",
    )(h, w, pc, pc, pc, pc, pc, pc, w_dw, b_dw, ln_g, ln_b)


def _swap_halves_mla(x, lane):
    return jnp.where(lane < MLA_NOPE + MLA_ROPE // 2,
                     pltpu.roll(x, LANES - MLA_ROPE // 2, 1), pltpu.roll(x, MLA_ROPE // 2, 1))


def _mla_prep_kernel(cq_ref, ckv_ref, cos_ref, sin_ref, qn_ref, kvn_ref, wq_ref, wqs_ref, wk_ref, wv_ref,
                     q_out, k_out, v_out):
    tm = cq_ref.shape[0]
    lane = lax.broadcasted_iota(jnp.int32, (tm, LANES), 1)
    cos = cos_ref[...]
    sin = sin_ref[...]
    scale = (MLA_NOPE + MLA_ROPE) ** -0.5 * LOG2_E
    cos_q = cos * scale
    sin_q = sin * scale

    hq = _rms(cq_ref[...].astype(f32), qn_ref[...]).astype(bf16)
    q = jnp.dot(hq, wq_ref[...], preferred_element_type=f32)
    q_sw = jnp.dot(hq, wqs_ref[...], preferred_element_type=f32)
    for h in range(MLA_HEADS):
        cols = slice(h * LANES, (h + 1) * LANES)
        q_out[0, h] = (q[:, cols] * cos_q + q_sw[:, cols] * sin_q).astype(bf16)

    blk = ckv_ref[...].astype(f32)
    hkv = _rms(blk[:, :MLA_KV_RANK], kvn_ref[...]).astype(bf16)
    kpe = blk[:, MLA_KV_RANK:]
    kpe = kpe * cos + _swap_halves_mla(kpe, lane) * sin
    kk = jnp.dot(hkv, wk_ref[...], preferred_element_type=f32)
    vv = jnp.dot(hkv, wv_ref[...], preferred_element_type=f32)
    ones_col = (lane == MLA_V).astype(f32)
    for h in range(MLA_HEADS):
        k_out[0, h] = (kk[:, h * LANES:(h + 1) * LANES] + kpe).astype(bf16)
        v_out[0, h] = (vv[:, h * LANES:(h + 1) * LANES] + ones_col).astype(bf16)


def _mla_prep(proj, cos_m, sin_m, qn, kvn, wq, wq_sw, wk, wv, l, B, S):
    tm = min(512, S)
    nS = S // tm
    hm = jax.ShapeDtypeStruct((B, MLA_HEADS, S, LANES), bf16)
    hm_spec = pl.BlockSpec((1, MLA_HEADS, tm, LANES), lambda b, i: (b, 0, i, 0))
    full = lambda shape: _resident(shape, l)
    return pl.pallas_call(
        _mla_prep_kernel,
        grid=(B, nS),
        in_specs=[pl.BlockSpec((tm, MLA_Q_RANK), lambda b, i: (b * nS + i, OFF_CQ // MLA_Q_RANK)),
                  pl.BlockSpec((tm, 2 * LANES), lambda b, i: (b * nS + i, OFF_CKV // (2 * LANES))),
                  pl.BlockSpec((tm, LANES), lambda b, i: (b * nS + i, 0)),
                  pl.BlockSpec((tm, LANES), lambda b, i: (b * nS + i, 0)),
                  full((1, MLA_Q_RANK)), full((1, MLA_KV_RANK)),
                  full((MLA_Q_RANK, MLA_HEADS * LANES)), full((MLA_Q_RANK, MLA_HEADS * LANES)),
                  full((MLA_KV_RANK, MLA_HEADS * LANES)),
                  full((MLA_KV_RANK, MLA_HEADS * LANES))],
        out_specs=[hm_spec, hm_spec, hm_spec],
        out_shape=[hm, hm, hm],
        compiler_params=_params("parallel", "parallel"),
        name="mla_prep",
    )(proj, proj, cos_m, sin_m, qn, kvn, wq, wq_sw, wk, wv)


def _attn_kernel(q_ref, k_ref, v_ref, o_ref, *, tk):
    tq = q_ref.shape[2]
    S = k_ref.shape[2]
    heads = q_ref.shape[1]
    qs = [q_ref[0, hh] for hh in range(heads)]

    def body(j, carry):
        start = pl.multiple_of(j * tk, tk)
        new = []
        for hh in range(heads):
            m, acc = carry[hh]
            ks = k_ref[0, hh, pl.ds(start, tk), :]
            vs = v_ref[0, hh, pl.ds(start, tk), :]
            s = lax.dot_general(qs[hh], ks, (((1,), (1,)), ((), ())), preferred_element_type=f32)
            m_new = jnp.maximum(m, jnp.max(s, axis=-1, keepdims=True))
            p = jnp.exp2(s - m_new)
            acc = acc * jnp.exp2(m - m_new) + jnp.dot(p.astype(bf16), vs, preferred_element_type=f32)
            new.append((m_new, acc))
        return tuple(new)

    init = tuple((jnp.full((tq, 1), -1e30, f32), jnp.zeros((tq, LANES), f32)) for _ in range(heads))
    res = lax.fori_loop(0, S // tk, body, init, unroll=True)
    outs = [acc / acc[:, MLA_V:MLA_V + 1] for _, acc in res]
    lane = lax.broadcasted_iota(jnp.int32, (tq, LANES), 1)
    for p in range(heads // 2):
        pair = jnp.where(lane < MLA_V, outs[2 * p], pltpu.roll(outs[2 * p + 1], MLA_V, 1))
        o_ref[0, :, p * LANES:(p + 1) * LANES] = pair.astype(bf16)


def _attention(q, k, v, B, S):
    tq = min(1024, S)
    tk = min(2048, S)
    hs = ATTN_HEADS_PER_STEP
    return pl.pallas_call(
        functools.partial(_attn_kernel, tk=tk),
        grid=(B, MLA_HEADS // hs, S // tq),
        in_specs=[pl.BlockSpec((1, hs, tq, LANES), lambda b, h, i: (b, h, i, 0)),
                  pl.BlockSpec((1, hs, S, LANES), lambda b, h, i: (b, h, 0, 0)),
                  pl.BlockSpec((1, hs, S, LANES), lambda b, h, i: (b, h, 0, 0))],
        out_specs=pl.BlockSpec((1, tq, hs * MLA_V), lambda b, h, i: (b, i, h)),
        out_shape=jax.ShapeDtypeStruct((B, S, MLA_HEADS * MLA_V), bf16),
        compiler_params=_params("parallel", "parallel", "arbitrary"),
        name="mla_attention",
    )(q, k, v)


CONV_ROWS = 64


def _conv_prepare(a_ref, g_ref, ap_ref, gp_ref, an_ref, gn_ref, ext_ref, sh_ref, i, last):
    ts = a_ref.shape[0]

    def glu(a, g):
        return a[...].astype(f32) * _sigmoid(g[...].astype(f32))

    ext_ref[HALO:HALO + ts, :] = glu(a_ref, g_ref)
    prev = glu(ap_ref, gp_ref)
    ext_ref[0:HALO, :] = jnp.where(i > 0, prev, 0.0)
    nxt = glu(an_ref, gn_ref)
    ext_ref[HALO + ts:2 * HALO + ts, :] = jnp.where(i < last, nxt, 0.0)
    n_sh = sh_ref.shape[1]
    for r in range(1, SUBLANES):
        sh_ref[r - 1] = ext_ref[r:r + n_sh, :]


def _conv_rows(r0, rows, w_ref, b_ref, lng_ref, lnb_ref, o_ref, ext_ref, sh_ref):
    first = HALO - CONV_WIDTH // 2
    acc = jnp.broadcast_to(b_ref[...], (rows // SUBLANES, SUBLANES, CONV_CH))
    for j in range(CONV_WIDTH):
        r = (first + j) % SUBLANES
        base = pl.ds(r0 + (first + j - r), rows)
        tap = ext_ref[base, :] if r == 0 else sh_ref[r - 1, base, :]
        acc = acc + tap.reshape(rows // SUBLANES, SUBLANES, CONV_CH) * w_ref[j]
    acc = acc.reshape(rows, CONV_CH)
    mu = jnp.mean(acc, axis=-1, keepdims=True)
    d = acc - mu
    var = jnp.mean(d * d, axis=-1, keepdims=True)
    y = d * lax.rsqrt(var + EPS) * lng_ref[...] + lnb_ref[...]
    o_ref[pl.ds(r0, rows), :] = (y * _sigmoid(y)).astype(bf16)


def _conv_specs(ts, nS, n_rows, l):
    hb = ts // HALO
    nhb = n_rows // HALO
    ca, cg = 0, 1

    def main(col):
        return pl.BlockSpec((ts, CONV_CH), lambda b, i: (b * nS + i, col))

    def prev(col):
        return pl.BlockSpec((HALO, CONV_CH), lambda b, i: (jnp.maximum((b * nS + i) * hb - 1, 0), col))

    def nxt(col):
        return pl.BlockSpec((HALO, CONV_CH), lambda b, i: (jnp.minimum((b * nS + i + 1) * hb, nhb - 1), col))

    vec = _resident((1, CONV_CH), l)
    return [main(ca), main(cg), prev(ca), prev(cg), nxt(ca), nxt(cg),
            _resident((CONV_WIDTH, SUBLANES, CONV_CH), l), vec, vec, vec]


def _ret_kernel(q_ref, k_ref, v_ref, g_ref, cos_ref, sin_ref, lg_ref, gn_ref, o_ref,
                kb, sall, ubuf):
    C = RET_CHUNK
    S = q_ref.shape[0]
    NC = S // C
    lgt = lg_ref[0]
    ls = jnp.minimum(lgt, 0.0) - jnp.log(1.0 + jnp.exp(-jnp.abs(lgt)))
    lgf = ls[0:1, :LANES]
    lgb = ls[1:2, :LANES]
    ri = lax.broadcasted_iota(jnp.int32, (C, LANES), 0).astype(f32)
    ci = lax.broadcasted_iota(jnp.int32, (C, LANES), 1).astype(f32)
    diff = ri - ci
    dmat = jnp.where(diff >= 0, jnp.exp(jnp.maximum(diff, 0.0) * lgf), jnp.exp(jnp.maximum(-diff, 0.0) * lgb))
    qf_dec = jnp.exp((ri + 1.0) * lgf)
    qb_dec = jnp.exp((C - ri) * lgb)
    kf_dec = jnp.exp((C - 1.0 - ri) * lgf)
    kb_dec = jnp.exp(ri * lgb)
    cdf = jnp.exp(C * ls[0:1, :])
    cdb = jnp.exp(C * ls[1:2, :])
    kscale = RET_DK ** -0.5
    contract0 = (((0,), (0,)), ((), ()))

    def chunk_rows(c):
        return pl.ds(pl.multiple_of(c * C, C), C)

    def rope(x, rows):
        x = x.astype(f32)
        return x * cos_ref[rows, :] + pltpu.roll(x, RET_DK // 2, 1) * sin_ref[rows, :]

    def increments(c, _):
        rows = chunk_rows(c)
        kr = rope(k_ref[rows, :], rows) * kscale
        kb[rows, :] = kr.astype(bf16)
        kcat = jnp.concatenate([(kr * kf_dec).astype(bf16), (kr * kb_dec).astype(bf16)], axis=1)
        ubuf[c] = lax.dot_general(kcat, v_ref[rows, :], contract0, preferred_element_type=f32)
        return 0

    lax.fori_loop(0, NC, increments, 0, unroll=8)

    def scan_fwd(c, sf):
        sall[c, 0:C, :] = sf.astype(bf16)
        return cdf * sf + ubuf[c, 0:C, :]

    def scan_bwd(t, sb):
        c = NC - 1 - t
        sall[c, C:2 * C, :] = sb.astype(bf16)
        return cdb * sb + ubuf[c, C:2 * C, :]

    zero_state = jnp.zeros((RET_DK, RET_DV), f32)
    lax.fori_loop(0, NC, scan_fwd, zero_state, unroll=2)
    lax.fori_loop(0, NC, scan_bwd, zero_state, unroll=2)

    gn = gn_ref[...]

    def outputs(c, _):
        rows = chunk_rows(c)
        qr = rope(q_ref[rows, :], rows)
        s = lax.dot_general(qr.astype(bf16), kb[rows, :], (((1,), (1,)), ((), ())),
                            preferred_element_type=f32) * dmat
        o = jnp.dot(s.astype(bf16), v_ref[rows, :], preferred_element_type=f32)
        qcat = jnp.concatenate([(qr * qf_dec).astype(bf16), (qr * qb_dec).astype(bf16)], axis=1)
        o = o + jnp.dot(qcat, sall[c], preferred_element_type=f32)
        mu = jnp.mean(o, axis=-1, keepdims=True)
        d = o - mu
        var = jnp.mean(d * d, axis=-1, keepdims=True)
        on = d * lax.rsqrt(var + EPS) * gn
        g = g_ref[rows, :].astype(f32)
        o_ref[rows, :] = (g * _sigmoid(g) * on).astype(bf16)
        return 0

    lax.fori_loop(0, NC, outputs, 0, unroll=8)


def _ret_branch(proj, cos_r, sin_r, lg, gn_g, l, B, S):
    nq = OFF_RQ // RET_DK
    nk = OFF_RK // RET_DK
    nv = OFF_RV // RET_DV
    ng = OFF_RG // RET_DV
    return pl.pallas_call(
        _ret_kernel,
        grid=(B, RET_HEADS),
        in_specs=[pl.BlockSpec((S, RET_DK), lambda b, h: (b, nq + h)),
                  pl.BlockSpec((S, RET_DK), lambda b, h: (b, nk + h)),
                  pl.BlockSpec((S, RET_DV), lambda b, h: (b, nv + h)),
                  pl.BlockSpec((S, RET_DV), lambda b, h: (b, ng + h)),
                  pl.BlockSpec((S, LANES), lambda b, h: (b, 0)),
                  pl.BlockSpec((S, LANES), lambda b, h: (b, 0)),
                  pl.BlockSpec((None, 1, 2, RET_DV), lambda b, h: (l, h, 0, 0)),
                  pl.BlockSpec((None, 1, RET_DV), lambda b, h: (l, 0, h))],
        out_specs=pl.BlockSpec((S, RET_DV), lambda b, h: (b, h)),
        out_shape=jax.ShapeDtypeStruct((B * S, RET_HEADS * RET_DV), bf16),
        scratch_shapes=[pltpu.VMEM((S, RET_DK), bf16),
                        pltpu.VMEM((S // RET_CHUNK, 2 * RET_DK, RET_DV), bf16),
                        pltpu.VMEM((S // RET_CHUNK, 2 * RET_DK, RET_DV), f32)],
        compiler_params=_params("parallel", "parallel"),
        name="retention",
    )(proj, proj, proj, proj, cos_r, sin_r, lg, gn_g)


def _merge_kernel(x_ref, g0_ref, g1_ref, g2_ref, om_ref, oc_ref, or_ref,
                  wm_ref, wc_ref, wr_ref, wo_ref, ln_ref, o_ref):
    tm = x_ref.shape[0]
    rb = min(MERGE_ROWS, tm)
    for r in range(tm // rb):
        rows = slice(r * rb, (r + 1) * rb)
        gate = lambda ref: _sigmoid(ref[rows, :].astype(f32))
        merged = gate(g0_ref) * jnp.dot(om_ref[rows, :], wm_ref[...], preferred_element_type=f32)
        merged = merged + gate(g1_ref) * jnp.dot(oc_ref[rows, :], wc_ref[...], preferred_element_type=f32)
        merged = merged + gate(g2_ref) * jnp.dot(or_ref[rows, :], wr_ref[...], preferred_element_type=f32)
        z = jnp.dot(merged.astype(bf16), wo_ref[...], preferred_element_type=f32)
        o_ref[rows, :] = x_ref[rows, :] + _rms(z, ln_ref[...])


def _merge(x2, proj, o_mla, o_conv, o_ret, w_mla, w_pw, w_ret, w_out, ln_post, l):
    T = x2.shape[0]
    tm = min(2 * MERGE_ROWS, T)
    row = lambda n, col=0: pl.BlockSpec((tm, n), lambda i: (i, col))
    return pl.pallas_call(
        _merge_kernel,
        grid=(T // tm,),
        in_specs=[row(D_MODEL), row(D_MODEL, 0), row(D_MODEL, 1), row(D_MODEL, 2),
                  row(MLA_HEADS * MLA_V), row(CONV_CH), row(RET_HEADS * RET_DV),
                  _resident((MLA_HEADS * MLA_V, D_MODEL), l), _resident((CONV_CH, D_MODEL), l),
                  _resident((RET_HEADS * RET_DV, D_MODEL), l), _resident((D_MODEL, D_MODEL), l),
                  _resident((1, D_MODEL), l)],
        out_specs=row(D_MODEL),
        out_shape=jax.ShapeDtypeStruct((T, D_MODEL), f32),
        compiler_params=_params("parallel"),
        name="merge_out",
    )(x2, proj, proj, proj, o_mla, o_conv, o_ret, w_mla, w_pw, w_ret, w_out, ln_post)


def _ffn_kernel(x_ref, gpre_ref, wg_ref, wu_ref, wd_ref, gpost_ref, o_ref):
    x = x_ref[...]
    h = _rms(x, gpre_ref[...]).astype(bf16)
    a = jnp.dot(h, wg_ref[...], preferred_element_type=f32)
    u = jnp.dot(h, wu_ref[...], preferred_element_type=f32)
    act = (a * _sigmoid(a) * u).astype(bf16)
    f = jnp.dot(act, wd_ref[...], preferred_element_type=f32)
    o_ref[...] = x + _rms(f, gpost_ref[...])


def _ffn(x2, g_pre, wg, wu, wd, g_post, l):
    T = x2.shape[0]
    tm = min(512, T)
    row = pl.BlockSpec((tm, D_MODEL), lambda i: (i, 0))
    return pl.pallas_call(
        _ffn_kernel,
        grid=(T // tm,),
        in_specs=[row, _resident((1, D_MODEL), l), _resident((D_MODEL, FFN_HIDDEN), l),
                  _resident((D_MODEL, FFN_HIDDEN), l), _resident((FFN_HIDDEN, D_MODEL), l),
                  _resident((1, D_MODEL), l)],
        out_specs=row,
        out_shape=jax.ShapeDtypeStruct((T, D_MODEL), f32),
        compiler_params=_params("parallel"),
        name="ffn",
    )(x2, g_pre, wg, wu, wd, g_post)


def _layout_w_in(w):
    sizes = (MLA_Q_RANK, MLA_KV_RANK, MLA_ROPE, CONV_CH, CONV_CH, RET_HEADS * RET_DK, RET_HEADS * RET_DK,
             RET_HEADS * RET_DV, RET_HEADS * RET_DV, N_BRANCH * D_MODEL)
    w = w.astype(bf16)
    pieces = []
    start = 0
    for n in sizes:
        pieces.append(w[:, :, start:start + n])
        start += n
    c_q, c_kv, k_pe, c_a, c_g, r_q, r_k, r_v, r_g, gates = pieces
    lead = w.shape[:2]
    kpe_blk = jnp.concatenate([jnp.zeros(lead + (MLA_NOPE,), w.dtype), k_pe,
                               jnp.zeros(lead + (LANES - MLA_NOPE - MLA_ROPE,), w.dtype)], axis=2)
    main = jnp.concatenate([gates, c_q, c_kv, kpe_blk, r_q, r_k, r_v, r_g], axis=2)
    conv = jnp.concatenate([c_a, c_g], axis=2)
    assert main.shape[2] == PROJ_COLS and conv.shape[2] == CONV_COLS
    return main, conv


def _layout_heads(w):
    depth, rows, heads, width = w.shape
    w = jnp.pad(w, ((0, 0), (0, 0), (0, 0), (0, LANES - width)))
    return w.reshape(depth, rows, heads * LANES).astype(bf16)


def _rows(a):
    return a.reshape(a.shape[0], 1, a.shape[1])


def kernel(x, positions, ln_mix_pre, ln_mix_post, ln_ffn_pre, ln_ffn_post, w_in, mla_q_norm, mla_w_uq,
           mla_kv_norm, mla_w_ukv, mla_w_o, conv_w_dw, conv_b_dw, conv_ln_g, conv_ln_b, conv_w_pw,
           ret_decay_logits, ret_gn_g, ret_w_o, w_out, ffn_w_gate, ffn_w_up, ffn_w_down):
    B, S, _ = x.shape
    depth = w_in.shape[0]
    T = B * S
    cos_m, sin_m, cos_r, sin_r = _rope_tables(positions)

    w_main, w_conv = _layout_w_in(w_in)
    w_uq = mla_w_uq.reshape(depth, MLA_Q_RANK, MLA_HEADS, MLA_NOPE + MLA_ROPE)
    wq = _layout_heads(w_uq)
    mid = MLA_NOPE + MLA_ROPE // 2
    wq_sw = _layout_heads(jnp.concatenate([w_uq[..., :MLA_NOPE], w_uq[..., mid:], w_uq[..., MLA_NOPE:mid]], -1))
    w_ukv = mla_w_ukv.reshape(depth, MLA_KV_RANK, MLA_HEADS, MLA_NOPE + MLA_V)
    wk = _layout_heads(w_ukv[..., :MLA_NOPE])
    wv = _layout_heads(w_ukv[..., MLA_NOPE:])
    conv_w = (jnp.broadcast_to(conv_w_dw[:, :, None, :], (depth, CONV_WIDTH, SUBLANES, CONV_CH)),
              _rows(conv_b_dw), _rows(conv_ln_g), _rows(conv_ln_b))
    lg = jnp.broadcast_to(jnp.swapaxes(ret_decay_logits, 1, 2)[..., None], (depth, RET_HEADS, 2, RET_DV))
    w_mla, w_pw, w_ret, w_o = (a.astype(bf16) for a in (mla_w_o, conv_w_pw, ret_w_o, w_out))
    wg, wu, wd = (a.astype(bf16) for a in (ffn_w_gate, ffn_w_up, ffn_w_down))

    x2 = x.reshape(T, D_MODEL)
    for l in range(depth):
        h, proj_conv = _norm_conv_proj(x2, _rows(ln_mix_pre), w_conv, l)
        proj, o_conv = _inproj_conv(h, w_main, proj_conv, conv_w, l, B, S)
        q, k, v = _mla_prep(proj, cos_m, sin_m, _rows(mla_q_norm), _rows(mla_kv_norm), wq, wq_sw, wk, wv, l, B, S)
        o_mla = _attention(q, k, v, B, S).reshape(T, MLA_HEADS * MLA_V)
        o_ret = _ret_branch(proj, cos_r, sin_r, lg, _rows(ret_gn_g), l, B, S)
        x2 = _merge(x2, proj, o_mla, o_conv, o_ret, w_mla, w_pw, w_ret, w_o, _rows(ln_mix_post), l)
        x2 = _ffn(x2, _rows(ln_ffn_pre), wg, wu, wd, _rows(ln_ffn_post), l)
    return x2.reshape(B, S, D_MODEL)
```

```python
import functools

import jax
import jax.numpy as jnp
from jax import lax
from jax.experimental import pallas as pl
from jax.experimental.pallas import tpu as pltpu

D_MODEL = 1024
MLA_HEADS = 8
MLA_NOPE = 64
MLA_ROPE = 32
MLA_V = 64
MLA_Q_RANK = 256
MLA_KV_RANK = 128
CONV_CH = 512
CONV_WIDTH = 31
RET_HEADS = 4
RET_DK = 128
RET_DV = 256
RET_CHUNK = 128
FFN_HIDDEN = 2816
N_BRANCH = 3
ROPE_BASE = 10000.0
EPS = 1e-6
LOG2_E = 1.4426950408889634

LANES = 128
SUBLANES = 8
ATTN_HEADS_PER_STEP = 4
MERGE_ROWS = 512
HALO = 16

OFF_GATE = 0
OFF_CQ = OFF_GATE + N_BRANCH * D_MODEL
OFF_CKV = OFF_CQ + MLA_Q_RANK
OFF_RQ = OFF_CKV + 2 * LANES
OFF_RK = OFF_RQ + RET_HEADS * RET_DK
OFF_RV = OFF_RK + RET_HEADS * RET_DK
OFF_RG = OFF_RV + RET_HEADS * RET_DV
PROJ_COLS = OFF_RG + RET_HEADS * RET_DV
CONV_COLS = 2 * CONV_CH
assert PROJ_COLS % 256 == 0

VMEM_LIMIT = 56 * 1024 * 1024

f32 = jnp.float32
bf16 = jnp.bfloat16


def _params(*sem):
    return pltpu.CompilerParams(dimension_semantics=sem, vmem_limit_bytes=VMEM_LIMIT)


def _rms(x, g):
    return x * lax.rsqrt(jnp.mean(x * x, axis=-1, keepdims=True) + EPS) * g


def _sigmoid(x):
    return 1.0 / (1.0 + jnp.exp(-x))


def _rope_table_kernel(pos_ref, inv_ref, sign_ref, cm_ref, sm_ref, cr_ref, sr_ref):
    tm = pos_ref.shape[0]
    pos = pos_ref[...].astype(f32)

    hr = tm // 2
    half = RET_DK // 2
    low = lax.broadcasted_iota(jnp.int32, (hr, LANES), 1) < half
    ang = jnp.where(low, pos[0:hr], pos[hr:tm]) * inv_ref[1:2, :]
    c = jnp.cos(ang)
    s = jnp.sin(ang)
    c_sw = pltpu.roll(c, half, 1)
    s_sw = pltpu.roll(s, half, 1)
    sign_r = sign_ref[1:2, :]
    cr_ref[0:hr, :] = jnp.where(low, c, c_sw)
    cr_ref[hr:tm, :] = jnp.where(low, c_sw, c)
    sr_ref[0:hr, :] = jnp.where(low, s, s_sw) * sign_r
    sr_ref[hr:tm, :] = jnp.where(low, s_sw, s) * sign_r

    nf = MLA_ROPE // 2
    groups = LANES // nf
    gm = tm // groups
    lane = lax.broadcasted_iota(jnp.int32, (gm, LANES), 1)
    grp = lane // nf
    p = pos[(groups - 1) * gm:groups * gm]
    for g in range(groups - 2, -1, -1):
        p = jnp.where(grp == g, pos[g * gm:(g + 1) * gm], p)
    ang = p * inv_ref[0:1, :]
    c = jnp.cos(ang)
    s = jnp.sin(ang)
    first = (lane >= MLA_NOPE) & (lane < MLA_NOPE + nf)
    second = (lane >= MLA_NOPE + nf) & (lane < MLA_NOPE + MLA_ROPE)
    sign_m = sign_ref[0:1, :]

    def place(x, g, fill):
        def shifted(target):
            shift = (target - nf * g) % LANES
            return pltpu.roll(x, shift, 1) if shift else x
        return jnp.where(first, shifted(MLA_NOPE), jnp.where(second, shifted(MLA_NOPE + nf), fill))

    for g in range(groups):
        rows = slice(g * gm, (g + 1) * gm)
        cm_ref[rows, :] = place(c, g, 1.0)
        sm_ref[rows, :] = place(s, g, 0.0) * sign_m


def _rope_tables(positions):
    T = positions.size
    tm = min(2048, T)
    inv_m = ROPE_BASE ** (-jnp.arange(0, MLA_ROPE, 2, dtype=f32) / MLA_ROPE)
    inv_r = ROPE_BASE ** (-jnp.arange(0, RET_DK, 2, dtype=f32) / RET_DK)
    half = MLA_ROPE // 2
    inv = jnp.stack([jnp.tile(inv_m, LANES // half), jnp.concatenate([inv_r, inv_r])])
    sign = jnp.stack([
        jnp.concatenate([jnp.ones((MLA_NOPE,), f32), -jnp.ones((half,), f32),
                         jnp.ones((LANES - MLA_NOPE - half,), f32)]),
        jnp.concatenate([-jnp.ones((RET_DK // 2,), f32), jnp.ones((RET_DK // 2,), f32)])])
    tab = jax.ShapeDtypeStruct((T, LANES), f32)
    small = pl.BlockSpec((2, LANES), lambda i: (0, 0))
    row = pl.BlockSpec((tm, LANES), lambda i: (i, 0))
    return pl.pallas_call(
        _rope_table_kernel,
        grid=(T // tm,),
        in_specs=[pl.BlockSpec((tm, 1), lambda i: (i, 0)), small, small],
        out_specs=[row, row, row, row],
        out_shape=[tab, tab, tab, tab],
        compiler_params=_params("parallel"),
        name="rope_tables",
    )(positions.reshape(T, 1), inv, sign)


def _resident(shape, l):
    return pl.BlockSpec((None,) + shape, lambda *_: (l,) + (0,) * len(shape), pipeline_mode=pl.Buffered(1))


def _norm_conv_proj_kernel(x_ref, g_ref, w_ref, h_ref, pc_ref):
    h = _rms(x_ref[...], g_ref[...]).astype(bf16)
    h_ref[...] = h
    pc_ref[...] = jnp.dot(h, w_ref[...], preferred_element_type=f32).astype(pc_ref.dtype)


def _norm_conv_proj(x2, g, w_conv, l):
    T = x2.shape[0]
    tm = min(1024, T)
    return pl.pallas_call(
        _norm_conv_proj_kernel,
        grid=(T // tm,),
        in_specs=[pl.BlockSpec((tm, D_MODEL), lambda i: (i, 0)), _resident((1, D_MODEL), l),
                  _resident((D_MODEL, CONV_COLS), l)],
        out_specs=[pl.BlockSpec((tm, D_MODEL), lambda i: (i, 0)),
                   pl.BlockSpec((tm, CONV_COLS), lambda i: (i, 0))],
        out_shape=[jax.ShapeDtypeStruct((T, D_MODEL), bf16), jax.ShapeDtypeStruct((T, CONV_COLS), bf16)],
        compiler_params=_params("parallel"),
        name="norm_conv_proj",
    )(x2, g, w_conv)


def _inproj_conv_kernel(h_ref, w_ref, a_ref, g_ref, ap_ref, gp_ref, an_ref, gn_ref,
                        wdw_ref, bdw_ref, lng_ref, lnb_ref, o_ref, oc_ref, ext_ref, sh_ref):
    ts = h_ref.shape[0]
    _conv_prepare(a_ref, g_ref, ap_ref, gp_ref, an_ref, gn_ref, ext_ref, sh_ref,
                  pl.program_id(1), pl.num_programs(1) - 1)
    o_ref[...] = jnp.dot(h_ref[...], w_ref[...], preferred_element_type=f32).astype(o_ref.dtype)
    rows = min(CONV_ROWS, ts)
    for k in range(ts // rows):
        _conv_rows(k * rows, rows, wdw_ref, bdw_ref, lng_ref, lnb_ref, oc_ref, ext_ref, sh_ref)


def _inproj_conv(h, w, pc, conv_w, l, B, S):
    ts = min(512, S)
    nS = S // ts
    w_dw, b_dw, ln_g, ln_b = conv_w
    row = lambda n: pl.BlockSpec((ts, n), lambda b, i: (b * nS + i, 0))
    return pl.pallas_call(
        _inproj_conv_kernel,
        grid=(B, nS),
        in_specs=[row(D_MODEL), _resident((D_MODEL, PROJ_COLS), l)] + _conv_specs(ts, nS, B * S, l),
        out_specs=[row(PROJ_COLS), row(CONV_CH)],
        out_shape=[jax.ShapeDtypeStruct((B * S, PROJ_COLS), bf16),
                   jax.ShapeDtypeStruct((B * S, CONV_CH), bf16)],
        scratch_shapes=[pltpu.VMEM((ts + 2 * HALO, CONV_CH), f32),
                        pltpu.VMEM((SUBLANES - 1, ts + 2 * HALO - SUBLANES, CONV_CH), f32)],
        compiler_params=_params("parallel", "parallel"),
        name="in_proj_conv",
    )(h, w, pc, pc, pc, pc, pc, pc, w_dw, b_dw, ln_g, ln_b)


def _swap_halves_mla(x, lane):
    return jnp.where(lane < MLA_NOPE + MLA_ROPE // 2,
                     pltpu.roll(x, LANES - MLA_ROPE // 2, 1), pltpu.roll(x, MLA_ROPE // 2, 1))


def _mla_prep_kernel(cq_ref, ckv_ref, cos_ref, sin_ref, qn_ref, kvn_ref, wq_ref, wqs_ref, wk_ref, wv_ref,
                     q_out, k_out, v_out):
    tm = cq_ref.shape[0]
    lane = lax.broadcasted_iota(jnp.int32, (tm, LANES), 1)
    cos = cos_ref[...]
    sin = sin_ref[...]
    scale = (MLA_NOPE + MLA_ROPE) ** -0.5 * LOG2_E
    cos_q = cos * scale
    sin_q = sin * scale

    hq = _rms(cq_ref[...].astype(f32), qn_ref[...]).astype(bf16)
    q = jnp.dot(hq, wq_ref[...], preferred_element_type=f32)
    q_sw = jnp.dot(hq, wqs_ref[...], preferred_element_type=f32)
    for h in range(MLA_HEADS):
        cols = slice(h * LANES, (h + 1) * LANES)
        q_out[0, h] = (q[:, cols] * cos_q + q_sw[:, cols] * sin_q).astype(bf16)

    blk = ckv_ref[...].astype(f32)
    hkv = _rms(blk[:, :MLA_KV_RANK], kvn_ref[...]).astype(bf16)
    kpe = blk[:, MLA_KV_RANK:]
    kpe = kpe * cos + _swap_halves_mla(kpe, lane) * sin
    kk = jnp.dot(hkv, wk_ref[...], preferred_element_type=f32)
    vv = jnp.dot(hkv, wv_ref[...], preferred_element_type=f32)
    ones_col = (lane == MLA_V).astype(f32)
    for h in range(MLA_HEADS):
        k_out[0, h] = (kk[:, h * LANES:(h + 1) * LANES] + kpe).astype(bf16)
        v_out[0, h] = (vv[:, h * LANES:(h + 1) * LANES] + ones_col).astype(bf16)


def _mla_prep(proj, cos_m, sin_m, qn, kvn, wq, wq_sw, wk, wv, l, B, S):
    tm = min(1024, S)
    nS = S // tm
    hm = jax.ShapeDtypeStruct((B, MLA_HEADS, S, LANES), bf16)
    hm_spec = pl.BlockSpec((1, MLA_HEADS, tm, LANES), lambda b, i: (b, 0, i, 0))
    full = lambda shape: _resident(shape, l)
    return pl.pallas_call(
        _mla_prep_kernel,
        grid=(B, nS),
        in_specs=[pl.BlockSpec((tm, MLA_Q_RANK), lambda b, i: (b * nS + i, OFF_CQ // MLA_Q_RANK)),
                  pl.BlockSpec((tm, 2 * LANES), lambda b, i: (b * nS + i, OFF_CKV // (2 * LANES))),
                  pl.BlockSpec((tm, LANES), lambda b, i: (b * nS + i, 0)),
                  pl.BlockSpec((tm, LANES), lambda b, i: (b * nS + i, 0)),
                  full((1, MLA_Q_RANK)), full((1, MLA_KV_RANK)),
                  full((MLA_Q_RANK, MLA_HEADS * LANES)), full((MLA_Q_RANK, MLA_HEADS * LANES)),
                  full((MLA_KV_RANK, MLA_HEADS * LANES)),
                  full((MLA_KV_RANK, MLA_HEADS * LANES))],
        out_specs=[hm_spec, hm_spec, hm_spec],
        out_shape=[hm, hm, hm],
        compiler_params=_params("parallel", "parallel"),
        name="mla_prep",
    )(proj, proj, cos_m, sin_m, qn, kvn, wq, wq_sw, wk, wv)


def _attn_kernel(q_ref, k_ref, v_ref, o_ref, *, tk):
    tq = q_ref.shape[2]
    S = k_ref.shape[2]
    heads = q_ref.shape[1]
    qs = [q_ref[0, hh] for hh in range(heads)]

    def body(j, carry):
        start = pl.multiple_of(j * tk, tk)
        new = []
        for hh in range(heads):
            m, acc = carry[hh]
            ks = k_ref[0, hh, pl.ds(start, tk), :]
            vs = v_ref[0, hh, pl.ds(start, tk), :]
            s = lax.dot_general(qs[hh], ks, (((1,), (1,)), ((), ())), preferred_element_type=f32)
            m_new = jnp.maximum(m, jnp.max(s, axis=-1, keepdims=True))
            p = jnp.exp2(s - m_new)
            acc = acc * jnp.exp2(m - m_new) + jnp.dot(p.astype(bf16), vs, preferred_element_type=f32)
            new.append((m_new, acc))
        return tuple(new)

    init = tuple((jnp.full((tq, 1), -1e30, f32), jnp.zeros((tq, LANES), f32)) for _ in range(heads))
    res = lax.fori_loop(0, S // tk, body, init, unroll=True)
    outs = [acc / acc[:, MLA_V:MLA_V + 1] for _, acc in res]
    lane = lax.broadcasted_iota(jnp.int32, (tq, LANES), 1)
    for p in range(heads // 2):
        pair = jnp.where(lane < MLA_V, outs[2 * p], pltpu.roll(outs[2 * p + 1], MLA_V, 1))
        o_ref[0, :, p * LANES:(p + 1) * LANES] = pair.astype(bf16)


def _attention(q, k, v, B, S):
    tq = min(1024, S)
    tk = min(2048, S)
    hs = ATTN_HEADS_PER_STEP
    return pl.pallas_call(
        functools.partial(_attn_kernel, tk=tk),
        grid=(B, MLA_HEADS // hs, S // tq),
        in_specs=[pl.BlockSpec((1, hs, tq, LANES), lambda b, h, i: (b, h, i, 0)),
                  pl.BlockSpec((1, hs, S, LANES), lambda b, h, i: (b, h, 0, 0)),
                  pl.BlockSpec((1, hs, S, LANES), lambda b, h, i: (b, h, 0, 0))],
        out_specs=pl.BlockSpec((1, tq, hs * MLA_V), lambda b, h, i: (b, i, h)),
        out_shape=jax.ShapeDtypeStruct((B, S, MLA_HEADS * MLA_V), bf16),
        compiler_params=_params("parallel", "parallel", "arbitrary"),
        name="mla_attention",
    )(q, k, v)


CONV_ROWS = 64


def _conv_prepare(a_ref, g_ref, ap_ref, gp_ref, an_ref, gn_ref, ext_ref, sh_ref, i, last):
    ts = a_ref.shape[0]

    def glu(a, g):
        return a[...].astype(f32) * _sigmoid(g[...].astype(f32))

    ext_ref[HALO:HALO + ts, :] = glu(a_ref, g_ref)
    prev = glu(ap_ref, gp_ref)
    ext_ref[0:HALO, :] = jnp.where(i > 0, prev, 0.0)
    nxt = glu(an_ref, gn_ref)
    ext_ref[HALO + ts:2 * HALO + ts, :] = jnp.where(i < last, nxt, 0.0)
    n_sh = sh_ref.shape[1]
    for r in range(1, SUBLANES):
        sh_ref[r - 1] = ext_ref[r:r + n_sh, :]


def _conv_rows(r0, rows, w_ref, b_ref, lng_ref, lnb_ref, o_ref, ext_ref, sh_ref):
    first = HALO - CONV_WIDTH // 2
    acc = jnp.broadcast_to(b_ref[...], (rows // SUBLANES, SUBLANES, CONV_CH))
    for j in range(CONV_WIDTH):
        r = (first + j) % SUBLANES
        base = pl.ds(r0 + (first + j - r), rows)
        tap = ext_ref[base, :] if r == 0 else sh_ref[r - 1, base, :]
        acc = acc + tap.reshape(rows // SUBLANES, SUBLANES, CONV_CH) * w_ref[j]
    acc = acc.reshape(rows, CONV_CH)
    mu = jnp.mean(acc, axis=-1, keepdims=True)
    d = acc - mu
    var = jnp.mean(d * d, axis=-1, keepdims=True)
    y = d * lax.rsqrt(var + EPS) * lng_ref[...] + lnb_ref[...]
    o_ref[pl.ds(r0, rows), :] = (y * _sigmoid(y)).astype(bf16)


def _conv_specs(ts, nS, n_rows, l):
    hb = ts // HALO
    nhb = n_rows // HALO
    ca, cg = 0, 1

    def main(col):
        return pl.BlockSpec((ts, CONV_CH), lambda b, i: (b * nS + i, col))

    def prev(col):
        return pl.BlockSpec((HALO, CONV_CH), lambda b, i: (jnp.maximum((b * nS + i) * hb - 1, 0), col))

    def nxt(col):
        return pl.BlockSpec((HALO, CONV_CH), lambda b, i: (jnp.minimum((b * nS + i + 1) * hb, nhb - 1), col))

    vec = _resident((1, CONV_CH), l)
    return [main(ca), main(cg), prev(ca), prev(cg), nxt(ca), nxt(cg),
            _resident((CONV_WIDTH, SUBLANES, CONV_CH), l), vec, vec, vec]


def _ret_kernel(q_ref, k_ref, v_ref, g_ref, cos_ref, sin_ref, lg_ref, gn_ref, o_ref,
                kb, sall, ubuf):
    C = RET_CHUNK
    S = q_ref.shape[0]
    NC = S // C
    lgt = lg_ref[0]
    ls = jnp.minimum(lgt, 0.0) - jnp.log(1.0 + jnp.exp(-jnp.abs(lgt)))
    lgf = ls[0:1, :LANES]
    lgb = ls[1:2, :LANES]
    ri = lax.broadcasted_iota(jnp.int32, (C, LANES), 0).astype(f32)
    ci = lax.broadcasted_iota(jnp.int32, (C, LANES), 1).astype(f32)
    diff = ri - ci
    dmat = jnp.where(diff >= 0, jnp.exp(jnp.maximum(diff, 0.0) * lgf), jnp.exp(jnp.maximum(-diff, 0.0) * lgb))
    qf_dec = jnp.exp((ri + 1.0) * lgf)
    qb_dec = jnp.exp((C - ri) * lgb)
    kf_dec = jnp.exp((C - 1.0 - ri) * lgf)
    kb_dec = jnp.exp(ri * lgb)
    cdf = jnp.exp(C * ls[0:1, :])
    cdb = jnp.exp(C * ls[1:2, :])
    kscale = RET_DK ** -0.5
    contract0 = (((0,), (0,)), ((), ()))

    def chunk_rows(c):
        return pl.ds(pl.multiple_of(c * C, C), C)

    def rope(x, rows):
        x = x.astype(f32)
        return x * cos_ref[rows, :] + pltpu.roll(x, RET_DK // 2, 1) * sin_ref[rows, :]

    def increments(c, _):
        rows = chunk_rows(c)
        kr = rope(k_ref[rows, :], rows) * kscale
        kb[rows, :] = kr.astype(bf16)
        kcat = jnp.concatenate([(kr * kf_dec).astype(bf16), (kr * kb_dec).astype(bf16)], axis=1)
        ubuf[c] = lax.dot_general(kcat, v_ref[rows, :], contract0, preferred_element_type=f32)
        return 0

    lax.fori_loop(0, NC, increments, 0, unroll=8)

    def scan_fwd(c, sf):
        sall[c, 0:C, :] = sf.astype(bf16)
        return cdf * sf + ubuf[c, 0:C, :]

    def scan_bwd(t, sb):
        c = NC - 1 - t
        sall[c, C:2 * C, :] = sb.astype(bf16)
        return cdb * sb + ubuf[c, C:2 * C, :]

    zero_state = jnp.zeros((RET_DK, RET_DV), f32)
    lax.fori_loop(0, NC, scan_fwd, zero_state, unroll=2)
    lax.fori_loop(0, NC, scan_bwd, zero_state, unroll=2)

    gn = gn_ref[...]

    def outputs(c, _):
        rows = chunk_rows(c)
        qr = rope(q_ref[rows, :], rows)
        s = lax.dot_general(qr.astype(bf16), kb[rows, :], (((1,), (1,)), ((), ())),
                            preferred_element_type=f32) * dmat
        o = jnp.dot(s.astype(bf16), v_ref[rows, :], preferred_element_type=f32)
        qcat = jnp.concatenate([(qr * qf_dec).astype(bf16), (qr * qb_dec).astype(bf16)], axis=1)
        o = o + jnp.dot(qcat, sall[c], preferred_element_type=f32)
        mu = jnp.mean(o, axis=-1, keepdims=True)
        d = o - mu
        var = jnp.mean(d * d, axis=-1, keepdims=True)
        on = d * lax.rsqrt(var + EPS) * gn
        g = g_ref[rows, :].astype(f32)
        o_ref[rows, :] = (g * _sigmoid(g) * on).astype(bf16)
        return 0

    lax.fori_loop(0, NC, outputs, 0, unroll=8)


def _ret_branch(proj, cos_r, sin_r, lg, gn_g, l, B, S):
    nq = OFF_RQ // RET_DK
    nk = OFF_RK // RET_DK
    nv = OFF_RV // RET_DV
    ng = OFF_RG // RET_DV
    return pl.pallas_call(
        _ret_kernel,
        grid=(B, RET_HEADS),
        in_specs=[pl.BlockSpec((S, RET_DK), lambda b, h: (b, nq + h)),
                  pl.BlockSpec((S, RET_DK), lambda b, h: (b, nk + h)),
                  pl.BlockSpec((S, RET_DV), lambda b, h: (b, nv + h)),
                  pl.BlockSpec((S, RET_DV), lambda b, h: (b, ng + h)),
                  pl.BlockSpec((S, LANES), lambda b, h: (b, 0)),
                  pl.BlockSpec((S, LANES), lambda b, h: (b, 0)),
                  pl.BlockSpec((None, 1, 2, RET_DV), lambda b, h: (l, h, 0, 0)),
                  pl.BlockSpec((None, 1, RET_DV), lambda b, h: (l, 0, h))],
        out_specs=pl.BlockSpec((S, RET_DV), lambda b, h: (b, h)),
        out_shape=jax.ShapeDtypeStruct((B * S, RET_HEADS * RET_DV), bf16),
        scratch_shapes=[pltpu.VMEM((S, RET_DK), bf16),
                        pltpu.VMEM((S // RET_CHUNK, 2 * RET_DK, RET_DV), bf16),
                        pltpu.VMEM((S // RET_CHUNK, 2 * RET_DK, RET_DV), f32)],
        compiler_params=_params("parallel", "parallel"),
        name="retention",
    )(proj, proj, proj, proj, cos_r, sin_r, lg, gn_g)


def _merge_kernel(x_ref, g0_ref, g1_ref, g2_ref, om_ref, oc_ref, or_ref,
                  wm_ref, wc_ref, wr_ref, wo_ref, ln_ref, o_ref):
    tm = x_ref.shape[0]
    rb = min(MERGE_ROWS, tm)
    for r in range(tm // rb):
        rows = slice(r * rb, (r + 1) * rb)
        gate = lambda ref: _sigmoid(ref[rows, :].astype(f32))
        merged = gate(g0_ref) * jnp.dot(om_ref[rows, :], wm_ref[...], preferred_element_type=f32)
        merged = merged + gate(g1_ref) * jnp.dot(oc_ref[rows, :], wc_ref[...], preferred_element_type=f32)
        merged = merged + gate(g2_ref) * jnp.dot(or_ref[rows, :], wr_ref[...], preferred_element_type=f32)
        z = jnp.dot(merged.astype(bf16), wo_ref[...], preferred_element_type=f32)
        o_ref[rows, :] = x_ref[rows, :] + _rms(z, ln_ref[...])


def _merge(x2, proj, o_mla, o_conv, o_ret, w_mla, w_pw, w_ret, w_out, ln_post, l):
    T = x2.shape[0]
    tm = min(2 * MERGE_ROWS, T)
    row = lambda n, col=0: pl.BlockSpec((tm, n), lambda i: (i, col))
    return pl.pallas_call(
        _merge_kernel,
        grid=(T // tm,),
        in_specs=[row(D_MODEL), row(D_MODEL, 0), row(D_MODEL, 1), row(D_MODEL, 2),
                  row(MLA_HEADS * MLA_V), row(CONV_CH), row(RET_HEADS * RET_DV),
                  _resident((MLA_HEADS * MLA_V, D_MODEL), l), _resident((CONV_CH, D_MODEL), l),
                  _resident((RET_HEADS * RET_DV, D_MODEL), l), _resident((D_MODEL, D_MODEL), l),
                  _resident((1, D_MODEL), l)],
        out_specs=row(D_MODEL),
        out_shape=jax.ShapeDtypeStruct((T, D_MODEL), f32),
        compiler_params=_params("parallel"),
        name="merge_out",
    )(x2, proj, proj, proj, o_mla, o_conv, o_ret, w_mla, w_pw, w_ret, w_out, ln_post)


def _ffn_kernel(x_ref, gpre_ref, wg_ref, wu_ref, wd_ref, gpost_ref, o_ref):
    x = x_ref[...]
    h = _rms(x, gpre_ref[...]).astype(bf16)
    a = jnp.dot(h, wg_ref[...], preferred_element_type=f32)
    u = jnp.dot(h, wu_ref[...], preferred_element_type=f32)
    act = (a * _sigmoid(a) * u).astype(bf16)
    f = jnp.dot(act, wd_ref[...], preferred_element_type=f32)
    o_ref[...] = x + _rms(f, gpost_ref[...])


def _ffn(x2, g_pre, wg, wu, wd, g_post, l):
    T = x2.shape[0]
    tm = min(512, T)
    row = pl.BlockSpec((tm, D_MODEL), lambda i: (i, 0))
    return pl.pallas_call(
        _ffn_kernel,
        grid=(T // tm,),
        in_specs=[row, _resident((1, D_MODEL), l), _resident((D_MODEL, FFN_HIDDEN), l),
                  _resident((D_MODEL, FFN_HIDDEN), l), _resident((FFN_HIDDEN, D_MODEL), l),
                  _resident((1, D_MODEL), l)],
        out_specs=row,
        out_shape=jax.ShapeDtypeStruct((T, D_MODEL), f32),
        compiler_params=_params("parallel"),
        name="ffn",
    )(x2, g_pre, wg, wu, wd, g_post)


def _layout_w_in(w):
    sizes = (MLA_Q_RANK, MLA_KV_RANK, MLA_ROPE, CONV_CH, CONV_CH, RET_HEADS * RET_DK, RET_HEADS * RET_DK,
             RET_HEADS * RET_DV, RET_HEADS * RET_DV, N_BRANCH * D_MODEL)
    w = w.astype(bf16)
    pieces = []
    start = 0
    for n in sizes:
        pieces.append(w[:, :, start:start + n])
        start += n
    c_q, c_kv, k_pe, c_a, c_g, r_q, r_k, r_v, r_g, gates = pieces
    lead = w.shape[:2]
    kpe_blk = jnp.concatenate([jnp.zeros(lead + (MLA_NOPE,), w.dtype), k_pe,
                               jnp.zeros(lead + (LANES - MLA_NOPE - MLA_ROPE,), w.dtype)], axis=2)
    main = jnp.concatenate([gates, c_q, c_kv, kpe_blk, r_q, r_k, r_v, r_g], axis=2)
    conv = jnp.concatenate([c_a, c_g], axis=2)
    assert main.shape[2] == PROJ_COLS and conv.shape[2] == CONV_COLS
    return main, conv


def _layout_heads(w):
    depth, rows, heads, width = w.shape
    w = jnp.pad(w, ((0, 0), (0, 0), (0, 0), (0, LANES - width)))
    return w.reshape(depth, rows, heads * LANES).astype(bf16)


def _rows(a):
    return a.reshape(a.shape[0], 1, a.shape[1])


def kernel(x, positions, ln_mix_pre, ln_mix_post, ln_ffn_pre, ln_ffn_post, w_in, mla_q_norm, mla_w_uq,
           mla_kv_norm, mla_w_ukv, mla_w_o, conv_w_dw, conv_b_dw, conv_ln_g, conv_ln_b, conv_w_pw,
           ret_decay_logits, ret_gn_g, ret_w_o, w_out, ffn_w_gate, ffn_w_up, ffn_w_down):
    B, S, _ = x.shape
    depth = w_in.shape[0]
    T = B * S
    cos_m, sin_m, cos_r, sin_r = _rope_tables(positions)

    w_main, w_conv = _layout_w_in(w_in)
    w_uq = mla_w_uq.reshape(depth, MLA_Q_RANK, MLA_HEADS, MLA_NOPE + MLA_ROPE)
    wq = _layout_heads(w_uq)
    mid = MLA_NOPE + MLA_ROPE // 2
    wq_sw = _layout_heads(jnp.concatenate([w_uq[..., :MLA_NOPE], w_uq[..., mid:], w_uq[..., MLA_NOPE:mid]], -1))
    w_ukv = mla_w_ukv.reshape(depth, MLA_KV_RANK, MLA_HEADS, MLA_NOPE + MLA_V)
    wk = _layout_heads(w_ukv[..., :MLA_NOPE])
    wv = _layout_heads(w_ukv[..., MLA_NOPE:])
    conv_w = (jnp.broadcast_to(conv_w_dw[:, :, None, :], (depth, CONV_WIDTH, SUBLANES, CONV_CH)),
              _rows(conv_b_dw), _rows(conv_ln_g), _rows(conv_ln_b))
    lg = jnp.broadcast_to(jnp.swapaxes(ret_decay_logits, 1, 2)[..., None], (depth, RET_HEADS, 2, RET_DV))
    w_mla, w_pw, w_ret, w_o = (a.astype(bf16) for a in (mla_w_o, conv_w_pw, ret_w_o, w_out))
    wg, wu, wd = (a.astype(bf16) for a in (ffn_w_gate, ffn_w_up, ffn_w_down))

    x2 = x.reshape(T, D_MODEL)
    for l in range(depth):
        h, proj_conv = _norm_conv_proj(x2, _rows(ln_mix_pre), w_conv, l)
        proj, o_conv = _inproj_conv(h, w_main, proj_conv, conv_w, l, B, S)
        q, k, v = _mla_prep(proj, cos_m, sin_m, _rows(mla_q_norm), _rows(mla_kv_norm), wq, wq_sw, wk, wv, l, B, S)
        o_mla = _attention(q, k, v, B, S).reshape(T, MLA_HEADS * MLA_V)
        o_ret = _ret_branch(proj, cos_r, sin_r, lg, _rows(ret_gn_g), l, B, S)
        x2 = _merge(x2, proj, o_mla, o_conv, o_ret, w_mla, w_pw, w_ret, w_o, _rows(ln_mix_post), l)
        x2 = _ffn(x2, _rows(ln_ffn_pre), wg, wu, wd, _rows(ln_ffn_post), l)
    return x2.reshape(B, S, D_MODEL)
```

```python
import functools

import jax
import jax.numpy as jnp
from jax import lax
from jax.experimental import pallas as pl
from jax.experimental.pallas import tpu as pltpu

D_MODEL = 1024
MLA_HEADS = 8
MLA_NOPE = 64
MLA_ROPE = 32
MLA_V = 64
MLA_Q_RANK = 256
MLA_KV_RANK = 128
CONV_CH = 512
CONV_WIDTH = 31
RET_HEADS = 4
RET_DK = 128
RET_DV = 256
RET_CHUNK = 128
FFN_HIDDEN = 2816
N_BRANCH = 3
ROPE_BASE = 10000.0
EPS = 1e-6
LOG2_E = 1.4426950408889634

LANES = 128
SUBLANES = 8
ATTN_HEADS_PER_STEP = 4
MERGE_ROWS = 512
HALO = 16

OFF_GATE = 0
OFF_CQ = OFF_GATE + N_BRANCH * D_MODEL
OFF_CKV = OFF_CQ + MLA_Q_RANK
OFF_RQ = OFF_CKV + 2 * LANES
OFF_RK = OFF_RQ + RET_HEADS * RET_DK
OFF_RV = OFF_RK + RET_HEADS * RET_DK
OFF_RG = OFF_RV + RET_HEADS * RET_DV
PROJ_COLS = OFF_RG + RET_HEADS * RET_DV
CONV_COLS = 2 * CONV_CH
assert PROJ_COLS % 256 == 0

VMEM_LIMIT = 56 * 1024 * 1024

f32 = jnp.float32
bf16 = jnp.bfloat16


def _params(*sem):
    return pltpu.CompilerParams(dimension_semantics=sem, vmem_limit_bytes=VMEM_LIMIT)


def _rms(x, g):
    return x * lax.rsqrt(jnp.mean(x * x, axis=-1, keepdims=True) + EPS) * g


def _sigmoid(x):
    return 1.0 / (1.0 + jnp.exp(-x))


def _rope_table_kernel(pos_ref, inv_ref, sign_ref, cm_ref, sm_ref, cr_ref, sr_ref):
    tm = pos_ref.shape[0]
    pos = pos_ref[...].astype(f32)

    hr = tm // 2
    half = RET_DK // 2
    low = lax.broadcasted_iota(jnp.int32, (hr, LANES), 1) < half
    ang = jnp.where(low, pos[0:hr], pos[hr:tm]) * inv_ref[1:2, :]
    c = jnp.cos(ang)
    s = jnp.sin(ang)
    c_sw = pltpu.roll(c, half, 1)
    s_sw = pltpu.roll(s, half, 1)
    sign_r = sign_ref[1:2, :]
    cr_ref[0:hr, :] = jnp.where(low, c, c_sw)
    cr_ref[hr:tm, :] = jnp.where(low, c_sw, c)
    sr_ref[0:hr, :] = jnp.where(low, s, s_sw) * sign_r
    sr_ref[hr:tm, :] = jnp.where(low, s_sw, s) * sign_r

    nf = MLA_ROPE // 2
    groups = LANES // nf
    gm = tm // groups
    lane = lax.broadcasted_iota(jnp.int32, (gm, LANES), 1)
    grp = lane // nf
    p = pos[(groups - 1) * gm:groups * gm]
    for g in range(groups - 2, -1, -1):
        p = jnp.where(grp == g, pos[g * gm:(g + 1) * gm], p)
    ang = p * inv_ref[0:1, :]
    c = jnp.cos(ang)
    s = jnp.sin(ang)
    first = (lane >= MLA_NOPE) & (lane < MLA_NOPE + nf)
    second = (lane >= MLA_NOPE + nf) & (lane < MLA_NOPE + MLA_ROPE)
    sign_m = sign_ref[0:1, :]

    def place(x, g, fill):
        def shifted(target):
            shift = (target - nf * g) % LANES
            return pltpu.roll(x, shift, 1) if shift else x
        return jnp.where(first, shifted(MLA_NOPE), jnp.where(second, shifted(MLA_NOPE + nf), fill))

    for g in range(groups):
        rows = slice(g * gm, (g + 1) * gm)
        cm_ref[rows, :] = place(c, g, 1.0)
        sm_ref[rows, :] = place(s, g, 0.0) * sign_m


def _rope_tables(positions):
    T = positions.size
    tm = min(2048, T)
    inv_m = ROPE_BASE ** (-jnp.arange(0, MLA_ROPE, 2, dtype=f32) / MLA_ROPE)
    inv_r = ROPE_BASE ** (-jnp.arange(0, RET_DK, 2, dtype=f32) / RET_DK)
    half = MLA_ROPE // 2
    inv = jnp.stack([jnp.tile(inv_m, LANES // half), jnp.concatenate([inv_r, inv_r])])
    sign = jnp.stack([
        jnp.concatenate([jnp.ones((MLA_NOPE,), f32), -jnp.ones((half,), f32),
                         jnp.ones((LANES - MLA_NOPE - half,), f32)]),
        jnp.concatenate([-jnp.ones((RET_DK // 2,), f32), jnp.ones((RET_DK // 2,), f32)])])
    tab = jax.ShapeDtypeStruct((T, LANES), f32)
    small = pl.BlockSpec((2, LANES), lambda i: (0, 0))
    row = pl.BlockSpec((tm, LANES), lambda i: (i, 0))
    return pl.pallas_call(
        _rope_table_kernel,
        grid=(T // tm,),
        in_specs=[pl.BlockSpec((tm, 1), lambda i: (i, 0)), small, small],
        out_specs=[row, row, row, row],
        out_shape=[tab, tab, tab, tab],
        compiler_params=_params("parallel"),
        name="rope_tables",
    )(positions.reshape(T, 1), inv, sign)


def _resident(shape, l):
    return pl.BlockSpec((None,) + shape, lambda *_: (l,) + (0,) * len(shape), pipeline_mode=pl.Buffered(1))


def _norm_conv_proj_kernel(x_ref, g_ref, w_ref, h_ref, pc_ref):
    h = _rms(x_ref[...], g_ref[...]).astype(bf16)
    h_ref[...] = h
    pc_ref[...] = jnp.dot(h, w_ref[...], preferred_element_type=f32).astype(pc_ref.dtype)


def _norm_conv_proj(x2, g, w_conv, l):
    T = x2.shape[0]
    tm = min(1024, T)
    return pl.pallas_call(
        _norm_conv_proj_kernel,
        grid=(T // tm,),
        in_specs=[pl.BlockSpec((tm, D_MODEL), lambda i: (i, 0)), _resident((1, D_MODEL), l),
                  _resident((D_MODEL, CONV_COLS), l)],
        out_specs=[pl.BlockSpec((tm, D_MODEL), lambda i: (i, 0)),
                   pl.BlockSpec((tm, CONV_COLS), lambda i: (i, 0))],
        out_shape=[jax.ShapeDtypeStruct((T, D_MODEL), bf16), jax.ShapeDtypeStruct((T, CONV_COLS), bf16)],
        compiler_params=_params("parallel"),
        name="norm_conv_proj",
    )(x2, g, w_conv)


def _inproj_conv_kernel(h_ref, w_ref, a_ref, g_ref, ap_ref, gp_ref, an_ref, gn_ref,
                        wdw_ref, bdw_ref, lng_ref, lnb_ref, o_ref, oc_ref, ext_ref, sh_ref):
    ts = h_ref.shape[0]
    _conv_prepare(a_ref, g_ref, ap_ref, gp_ref, an_ref, gn_ref, ext_ref, sh_ref,
                  pl.program_id(1), pl.num_programs(1) - 1)
    o_ref[...] = jnp.dot(h_ref[...], w_ref[...], preferred_element_type=f32).astype(o_ref.dtype)
    rows = min(CONV_ROWS, ts)
    for k in range(ts // rows):
        _conv_rows(k * rows, rows, wdw_ref, bdw_ref, lng_ref, lnb_ref, oc_ref, ext_ref, sh_ref)


def _inproj_conv(h, w, pc, conv_w, l, B, S):
    ts = min(512, S)
    nS = S // ts
    w_dw, b_dw, ln_g, ln_b = conv_w
    row = lambda n: pl.BlockSpec((ts, n), lambda b, i: (b * nS + i, 0))
    return pl.pallas_call(
        _inproj_conv_kernel,
        grid=(B, nS),
        in_specs=[row(D_MODEL), _resident((D_MODEL, PROJ_COLS), l)] + _conv_specs(ts, nS, B * S, l),
        out_specs=[row(PROJ_COLS), row(CONV_CH)],
        out_shape=[jax.ShapeDtypeStruct((B * S, PROJ_COLS), bf16),
                   jax.ShapeDtypeStruct((B * S, CONV_CH), bf16)],
        scratch_shapes=[pltpu.VMEM((ts + 2 * HALO, CONV_CH), f32),
                        pltpu.VMEM((SUBLANES - 1, ts + 2 * HALO - SUBLANES, CONV_CH), f32)],
        compiler_params=_params("parallel", "parallel"),
        name="in_proj_conv",
    )(h, w, pc, pc, pc, pc, pc, pc, w_dw, b_dw, ln_g, ln_b)


def _swap_halves_mla(x, lane):
    return jnp.where(lane < MLA_NOPE + MLA_ROPE // 2,
                     pltpu.roll(x, LANES - MLA_ROPE // 2, 1), pltpu.roll(x, MLA_ROPE // 2, 1))


def _mla_prep_kernel(cq_ref, ckv_ref, cos_ref, sin_ref, qn_ref, kvn_ref, wq_ref, wqs_ref, wk_ref, wv_ref,
                     q_out, k_out, v_out):
    tm = cq_ref.shape[0]
    lane = lax.broadcasted_iota(jnp.int32, (tm, LANES), 1)
    cos = cos_ref[...]
    sin = sin_ref[...]
    scale = (MLA_NOPE + MLA_ROPE) ** -0.5 * LOG2_E
    cos_q = cos * scale
    sin_q = sin * scale

    hq = _rms(cq_ref[...].astype(f32), qn_ref[...]).astype(bf16)
    q = jnp.dot(hq, wq_ref[...], preferred_element_type=f32)
    q_sw = jnp.dot(hq, wqs_ref[...], preferred_element_type=f32)
    for h in range(MLA_HEADS):
        cols = slice(h * LANES, (h + 1) * LANES)
        q_out[0, h] = (q[:, cols] * cos_q + q_sw[:, cols] * sin_q).astype(bf16)

    blk = ckv_ref[...].astype(f32)
    hkv = _rms(blk[:, :MLA_KV_RANK], kvn_ref[...]).astype(bf16)
    kpe = blk[:, MLA_KV_RANK:]
    kpe = kpe * cos + _swap_halves_mla(kpe, lane) * sin
    kk = jnp.dot(hkv, wk_ref[...], preferred_element_type=f32)
    vv = jnp.dot(hkv, wv_ref[...], preferred_element_type=f32)
    ones_col = (lane == MLA_V).astype(f32)
    for h in range(MLA_HEADS):
        k_out[0, h] = (kk[:, h * LANES:(h + 1) * LANES] + kpe).astype(bf16)
        v_out[0, h] = (vv[:, h * LANES:(h + 1) * LANES] + ones_col).astype(bf16)


def _mla_prep(proj, cos_m, sin_m, qn, kvn, wq, wq_sw, wk, wv, l, B, S):
    tm = min(1024, S)
    nS = S // tm
    hm = jax.ShapeDtypeStruct((B, MLA_HEADS, S, LANES), bf16)
    hm_spec = pl.BlockSpec((1, MLA_HEADS, tm, LANES), lambda b, i: (b, 0, i, 0))
    full = lambda shape: _resident(shape, l)
    return pl.pallas_call(
        _mla_prep_kernel,
        grid=(B, nS),
        in_specs=[pl.BlockSpec((tm, MLA_Q_RANK), lambda b, i: (b * nS + i, OFF_CQ // MLA_Q_RANK)),
                  pl.BlockSpec((tm, 2 * LANES), lambda b, i: (b * nS + i, OFF_CKV // (2 * LANES))),
                  pl.BlockSpec((tm, LANES), lambda b, i: (b * nS + i, 0)),
                  pl.BlockSpec((tm, LANES), lambda b, i: (b * nS + i, 0)),
                  full((1, MLA_Q_RANK)), full((1, MLA_KV_RANK)),
                  full((MLA_Q_RANK, MLA_HEADS * LANES)), full((MLA_Q_RANK, MLA_HEADS * LANES)),
                  full((MLA_KV_RANK, MLA_HEADS * LANES)),
                  full((MLA_KV_RANK, MLA_HEADS * LANES))],
        out_specs=[hm_spec, hm_spec, hm_spec],
        out_shape=[hm, hm, hm],
        compiler_params=_params("parallel", "parallel"),
        name="mla_prep",
    )(proj, proj, cos_m, sin_m, qn, kvn, wq, wq_sw, wk, wv)


def _attn_kernel(q_ref, k_ref, v_ref, o_ref, acc_ref, *, tk):
    tq = q_ref.shape[2]
    S = k_ref.shape[2]
    heads = q_ref.shape[1]
    ms = [None] * heads
    for j in range(S // tk):
        for hh in range(heads):
            ks = k_ref[0, hh, j * tk:(j + 1) * tk, :]
            vs = v_ref[0, hh, j * tk:(j + 1) * tk, :]
            s = lax.dot_general(q_ref[0, hh], ks, (((1,), (1,)), ((), ())), preferred_element_type=f32)
            m_blk = jnp.max(s, axis=-1, keepdims=True)
            if j == 0:
                ms[hh] = m_blk
                acc_ref[hh] = jnp.dot(jnp.exp2(s - m_blk).astype(bf16), vs, preferred_element_type=f32)
            else:
                m_old = ms[hh]
                m_new = jnp.maximum(m_old, m_blk)
                ms[hh] = m_new
                pv = jnp.dot(jnp.exp2(s - m_new).astype(bf16), vs, preferred_element_type=f32)
                acc_ref[hh] = acc_ref[hh] * jnp.exp2(m_old - m_new) + pv
    outs = []
    for hh in range(heads):
        acc = acc_ref[hh]
        outs.append(acc / acc[:, MLA_V:MLA_V + 1])
    lane = lax.broadcasted_iota(jnp.int32, (tq, LANES), 1)
    for p in range(heads // 2):
        pair = jnp.where(lane < MLA_V, outs[2 * p], pltpu.roll(outs[2 * p + 1], MLA_V, 1))
        o_ref[0, :, p * LANES:(p + 1) * LANES] = pair.astype(bf16)


def _attention(q, k, v, B, S):
    tq = min(1024, S)
    tk = min(2048, S)
    hs = ATTN_HEADS_PER_STEP
    return pl.pallas_call(
        functools.partial(_attn_kernel, tk=tk),
        grid=(B, MLA_HEADS // hs, S // tq),
        in_specs=[pl.BlockSpec((1, hs, tq, LANES), lambda b, h, i: (b, h, i, 0)),
                  pl.BlockSpec((1, hs, S, LANES), lambda b, h, i: (b, h, 0, 0)),
                  pl.BlockSpec((1, hs, S, LANES), lambda b, h, i: (b, h, 0, 0))],
        out_specs=pl.BlockSpec((1, tq, hs * MLA_V), lambda b, h, i: (b, i, h)),
        out_shape=jax.ShapeDtypeStruct((B, S, MLA_HEADS * MLA_V), bf16),
        scratch_shapes=[pltpu.VMEM((hs, tq, LANES), f32)],
        compiler_params=_params("parallel", "parallel", "arbitrary"),
        name="mla_attention",
    )(q, k, v)


CONV_ROWS = 64


def _conv_prepare(a_ref, g_ref, ap_ref, gp_ref, an_ref, gn_ref, ext_ref, sh_ref, i, last):
    ts = a_ref.shape[0]

    def glu(a, g):
        return a[...].astype(f32) * _sigmoid(g[...].astype(f32))

    ext_ref[HALO:HALO + ts, :] = glu(a_ref, g_ref)
    prev = glu(ap_ref, gp_ref)
    ext_ref[0:HALO, :] = jnp.where(i > 0, prev, 0.0)
    nxt = glu(an_ref, gn_ref)
    ext_ref[HALO + ts:2 * HALO + ts, :] = jnp.where(i < last, nxt, 0.0)
    n_sh = sh_ref.shape[1]
    for r in range(1, SUBLANES):
        sh_ref[r - 1] = ext_ref[r:r + n_sh, :]


def _conv_rows(r0, rows, w_ref, b_ref, lng_ref, lnb_ref, o_ref, ext_ref, sh_ref):
    first = HALO - CONV_WIDTH // 2
    acc = jnp.broadcast_to(b_ref[...], (rows // SUBLANES, SUBLANES, CONV_CH))
    for j in range(CONV_WIDTH):
        r = (first + j) % SUBLANES
        base = pl.ds(r0 + (first + j - r), rows)
        tap = ext_ref[base, :] if r == 0 else sh_ref[r - 1, base, :]
        acc = acc + tap.reshape(rows // SUBLANES, SUBLANES, CONV_CH) * w_ref[j]
    acc = acc.reshape(rows, CONV_CH)
    mu = jnp.mean(acc, axis=-1, keepdims=True)
    d = acc - mu
    var = jnp.mean(d * d, axis=-1, keepdims=True)
    y = d * lax.rsqrt(var + EPS) * lng_ref[...] + lnb_ref[...]
    o_ref[pl.ds(r0, rows), :] = (y * _sigmoid(y)).astype(bf16)


def _conv_specs(ts, nS, n_rows, l):
    hb = ts // HALO
    nhb = n_rows // HALO
    ca, cg = 0, 1

    def main(col):
        return pl.BlockSpec((ts, CONV_CH), lambda b, i: (b * nS + i, col))

    def prev(col):
        return pl.BlockSpec((HALO, CONV_CH), lambda b, i: (jnp.maximum((b * nS + i) * hb - 1, 0), col))

    def nxt(col):
        return pl.BlockSpec((HALO, CONV_CH), lambda b, i: (jnp.minimum((b * nS + i + 1) * hb, nhb - 1), col))

    vec = _resident((1, CONV_CH), l)
    return [main(ca), main(cg), prev(ca), prev(cg), nxt(ca), nxt(cg),
            _resident((CONV_WIDTH, SUBLANES, CONV_CH), l), vec, vec, vec]


def _ret_kernel(q_ref, k_ref, v_ref, g_ref, cos_ref, sin_ref, lg_ref, gn_ref, o_ref,
                kb, sall, ubuf):
    C = RET_CHUNK
    S = q_ref.shape[0]
    NC = S // C
    lgt = lg_ref[0]
    ls = jnp.minimum(lgt, 0.0) - jnp.log(1.0 + jnp.exp(-jnp.abs(lgt)))
    lgf = ls[0:1, :LANES]
    lgb = ls[1:2, :LANES]
    ri = lax.broadcasted_iota(jnp.int32, (C, LANES), 0).astype(f32)
    ci = lax.broadcasted_iota(jnp.int32, (C, LANES), 1).astype(f32)
    diff = ri - ci
    dmat = jnp.where(diff >= 0, jnp.exp(jnp.maximum(diff, 0.0) * lgf), jnp.exp(jnp.maximum(-diff, 0.0) * lgb))
    qf_dec = jnp.exp((ri + 1.0) * lgf)
    qb_dec = jnp.exp((C - ri) * lgb)
    kf_dec = jnp.exp((C - 1.0 - ri) * lgf)
    kb_dec = jnp.exp(ri * lgb)
    cdf = jnp.exp(C * ls[0:1, :])
    cdb = jnp.exp(C * ls[1:2, :])
    kscale = RET_DK ** -0.5
    contract0 = (((0,), (0,)), ((), ()))

    def chunk_rows(c):
        return pl.ds(pl.multiple_of(c * C, C), C)

    def rope(x, rows):
        x = x.astype(f32)
        return x * cos_ref[rows, :] + pltpu.roll(x, RET_DK // 2, 1) * sin_ref[rows, :]

    def increments(c, _):
        rows = chunk_rows(c)
        kr = rope(k_ref[rows, :], rows) * kscale
        kb[rows, :] = kr.astype(bf16)
        kcat = jnp.concatenate([(kr * kf_dec).astype(bf16), (kr * kb_dec).astype(bf16)], axis=1)
        ubuf[c] = lax.dot_general(kcat, v_ref[rows, :], contract0, preferred_element_type=f32)
        return 0

    lax.fori_loop(0, NC, increments, 0, unroll=8)

    def scan_fwd(c, sf):
        sall[c, 0:C, :] = sf.astype(bf16)
        return cdf * sf + ubuf[c, 0:C, :]

    def scan_bwd(t, sb):
        c = NC - 1 - t
        sall[c, C:2 * C, :] = sb.astype(bf16)
        return cdb * sb + ubuf[c, C:2 * C, :]

    zero_state = jnp.zeros((RET_DK, RET_DV), f32)
    lax.fori_loop(0, NC, scan_fwd, zero_state, unroll=2)
    lax.fori_loop(0, NC, scan_bwd, zero_state, unroll=2)

    gn = gn_ref[...]

    def outputs(c, _):
        rows = chunk_rows(c)
        qr = rope(q_ref[rows, :], rows)
        s = lax.dot_general(qr.astype(bf16), kb[rows, :], (((1,), (1,)), ((), ())),
                            preferred_element_type=f32) * dmat
        o = jnp.dot(s.astype(bf16), v_ref[rows, :], preferred_element_type=f32)
        qcat = jnp.concatenate([(qr * qf_dec).astype(bf16), (qr * qb_dec).astype(bf16)], axis=1)
        o = o + jnp.dot(qcat, sall[c], preferred_element_type=f32)
        mu = jnp.mean(o, axis=-1, keepdims=True)
        d = o - mu
        var = jnp.mean(d * d, axis=-1, keepdims=True)
        on = d * lax.rsqrt(var + EPS) * gn
        g = g_ref[rows, :].astype(f32)
        o_ref[rows, :] = (g * _sigmoid(g) * on).astype(bf16)
        return 0

    lax.fori_loop(0, NC, outputs, 0, unroll=8)


def _ret_branch(proj, cos_r, sin_r, lg, gn_g, l, B, S):
    nq = OFF_RQ // RET_DK
    nk = OFF_RK // RET_DK
    nv = OFF_RV // RET_DV
    ng = OFF_RG // RET_DV
    return pl.pallas_call(
        _ret_kernel,
        grid=(B, RET_HEADS),
        in_specs=[pl.BlockSpec((S, RET_DK), lambda b, h: (b, nq + h)),
                  pl.BlockSpec((S, RET_DK), lambda b, h: (b, nk + h)),
                  pl.BlockSpec((S, RET_DV), lambda b, h: (b, nv + h)),
                  pl.BlockSpec((S, RET_DV), lambda b, h: (b, ng + h)),
                  pl.BlockSpec((S, LANES), lambda b, h: (b, 0)),
                  pl.BlockSpec((S, LANES), lambda b, h: (b, 0)),
                  pl.BlockSpec((None, 1, 2, RET_DV), lambda b, h: (l, h, 0, 0)),
                  pl.BlockSpec((None, 1, RET_DV), lambda b, h: (l, 0, h))],
        out_specs=pl.BlockSpec((S, RET_DV), lambda b, h: (b, h)),
        out_shape=jax.ShapeDtypeStruct((B * S, RET_HEADS * RET_DV), bf16),
        scratch_shapes=[pltpu.VMEM((S, RET_DK), bf16),
                        pltpu.VMEM((S // RET_CHUNK, 2 * RET_DK, RET_DV), bf16),
                        pltpu.VMEM((S // RET_CHUNK, 2 * RET_DK, RET_DV), f32)],
        compiler_params=_params("parallel", "parallel"),
        name="retention",
    )(proj, proj, proj, proj, cos_r, sin_r, lg, gn_g)


def _merge_kernel(x_ref, g0_ref, g1_ref, g2_ref, om_ref, oc_ref, or_ref,
                  wm_ref, wc_ref, wr_ref, wo_ref, ln_ref, o_ref):
    tm = x_ref.shape[0]
    rb = min(MERGE_ROWS, tm)
    for r in range(tm // rb):
        rows = slice(r * rb, (r + 1) * rb)
        gate = lambda ref: _sigmoid(ref[rows, :].astype(f32))
        merged = gate(g0_ref) * jnp.dot(om_ref[rows, :], wm_ref[...], preferred_element_type=f32)
        merged = merged + gate(g1_ref) * jnp.dot(oc_ref[rows, :], wc_ref[...], preferred_element_type=f32)
        merged = merged + gate(g2_ref) * jnp.dot(or_ref[rows, :], wr_ref[...], preferred_element_type=f32)
        z = jnp.dot(merged.astype(bf16), wo_ref[...], preferred_element_type=f32)
        o_ref[rows, :] = x_ref[rows, :] + _rms(z, ln_ref[...])


def _merge(x2, proj, o_mla, o_conv, o_ret, w_mla, w_pw, w_ret, w_out, ln_post, l):
    T = x2.shape[0]
    tm = min(2 * MERGE_ROWS, T)
    row = lambda n, col=0: pl.BlockSpec((tm, n), lambda i: (i, col))
    return pl.pallas_call(
        _merge_kernel,
        grid=(T // tm,),
        in_specs=[row(D_MODEL), row(D_MODEL, 0), row(D_MODEL, 1), row(D_MODEL, 2),
                  row(MLA_HEADS * MLA_V), row(CONV_CH), row(RET_HEADS * RET_DV),
                  _resident((MLA_HEADS * MLA_V, D_MODEL), l), _resident((CONV_CH, D_MODEL), l),
                  _resident((RET_HEADS * RET_DV, D_MODEL), l), _resident((D_MODEL, D_MODEL), l),
                  _resident((1, D_MODEL), l)],
        out_specs=row(D_MODEL),
        out_shape=jax.ShapeDtypeStruct((T, D_MODEL), f32),
        compiler_params=_params("parallel"),
        name="merge_out",
    )(x2, proj, proj, proj, o_mla, o_conv, o_ret, w_mla, w_pw, w_ret, w_out, ln_post)


def _ffn_kernel(x_ref, gpre_ref, wg_ref, wu_ref, wd_ref, gpost_ref, o_ref):
    x = x_ref[...]
    h = _rms(x, gpre_ref[...]).astype(bf16)
    a = jnp.dot(h, wg_ref[...], preferred_element_type=f32)
    u = jnp.dot(h, wu_ref[...], preferred_element_type=f32)
    act = (a * _sigmoid(a) * u).astype(bf16)
    f = jnp.dot(act, wd_ref[...], preferred_element_type=f32)
    o_ref[...] = x + _rms(f, gpost_ref[...])


def _ffn(x2, g_pre, wg, wu, wd, g_post, l):
    T = x2.shape[0]
    tm = min(512, T)
    row = pl.BlockSpec((tm, D_MODEL), lambda i: (i, 0))
    return pl.pallas_call(
        _ffn_kernel,
        grid=(T // tm,),
        in_specs=[row, _resident((1, D_MODEL), l), _resident((D_MODEL, FFN_HIDDEN), l),
                  _resident((D_MODEL, FFN_HIDDEN), l), _resident((FFN_HIDDEN, D_MODEL), l),
                  _resident((1, D_MODEL), l)],
        out_specs=row,
        out_shape=jax.ShapeDtypeStruct((T, D_MODEL), f32),
        compiler_params=_params("parallel"),
        name="ffn",
    )(x2, g_pre, wg, wu, wd, g_post)


def _layout_w_in(w):
    sizes = (MLA_Q_RANK, MLA_KV_RANK, MLA_ROPE, CONV_CH, CONV_CH, RET_HEADS * RET_DK, RET_HEADS * RET_DK,
             RET_HEADS * RET_DV, RET_HEADS * RET_DV, N_BRANCH * D_MODEL)
    w = w.astype(bf16)
    pieces = []
    start = 0
    for n in sizes:
        pieces.append(w[:, :, start:start + n])
        start += n
    c_q, c_kv, k_pe, c_a, c_g, r_q, r_k, r_v, r_g, gates = pieces
    lead = w.shape[:2]
    kpe_blk = jnp.concatenate([jnp.zeros(lead + (MLA_NOPE,), w.dtype), k_pe,
                               jnp.zeros(lead + (LANES - MLA_NOPE - MLA_ROPE,), w.dtype)], axis=2)
    main = jnp.concatenate([gates, c_q, c_kv, kpe_blk, r_q, r_k, r_v, r_g], axis=2)
    conv = jnp.concatenate([c_a, c_g], axis=2)
    assert main.shape[2] == PROJ_COLS and conv.shape[2] == CONV_COLS
    return main, conv


def _layout_heads(w):
    depth, rows, heads, width = w.shape
    w = jnp.pad(w, ((0, 0), (0, 0), (0, 0), (0, LANES - width)))
    return w.reshape(depth, rows, heads * LANES).astype(bf16)


def _rows(a):
    return a.reshape(a.shape[0], 1, a.shape[1])


def kernel(x, positions, ln_mix_pre, ln_mix_post, ln_ffn_pre, ln_ffn_post, w_in, mla_q_norm, mla_w_uq,
           mla_kv_norm, mla_w_ukv, mla_w_o, conv_w_dw, conv_b_dw, conv_ln_g, conv_ln_b, conv_w_pw,
           ret_decay_logits, ret_gn_g, ret_w_o, w_out, ffn_w_gate, ffn_w_up, ffn_w_down):
    B, S, _ = x.shape
    depth = w_in.shape[0]
    T = B * S
    cos_m, sin_m, cos_r, sin_r = _rope_tables(positions)

    w_main, w_conv = _layout_w_in(w_in)
    w_uq = mla_w_uq.reshape(depth, MLA_Q_RANK, MLA_HEADS, MLA_NOPE + MLA_ROPE)
    wq = _layout_heads(w_uq)
    mid = MLA_NOPE + MLA_ROPE // 2
    wq_sw = _layout_heads(jnp.concatenate([w_uq[..., :MLA_NOPE], w_uq[..., mid:], w_uq[..., MLA_NOPE:mid]], -1))
    w_ukv = mla_w_ukv.reshape(depth, MLA_KV_RANK, MLA_HEADS, MLA_NOPE + MLA_V)
    wk = _layout_heads(w_ukv[..., :MLA_NOPE])
    wv = _layout_heads(w_ukv[..., MLA_NOPE:])
    conv_w = (jnp.broadcast_to(conv_w_dw[:, :, None, :], (depth, CONV_WIDTH, SUBLANES, CONV_CH)),
              _rows(conv_b_dw), _rows(conv_ln_g), _rows(conv_ln_b))
    lg = jnp.broadcast_to(jnp.swapaxes(ret_decay_logits, 1, 2)[..., None], (depth, RET_HEADS, 2, RET_DV))
    w_mla, w_pw, w_ret, w_o = (a.astype(bf16) for a in (mla_w_o, conv_w_pw, ret_w_o, w_out))
    wg, wu, wd = (a.astype(bf16) for a in (ffn_w_gate, ffn_w_up, ffn_w_down))

    x2 = x.reshape(T, D_MODEL)
    for l in range(depth):
        h, proj_conv = _norm_conv_proj(x2, _rows(ln_mix_pre), w_conv, l)
        proj, o_conv = _inproj_conv(h, w_main, proj_conv, conv_w, l, B, S)
        q, k, v = _mla_prep(proj, cos_m, sin_m, _rows(mla_q_norm), _rows(mla_kv_norm), wq, wq_sw, wk, wv, l, B, S)
        o_mla = _attention(q, k, v, B, S).reshape(T, MLA_HEADS * MLA_V)
        o_ret = _ret_branch(proj, cos_r, sin_r, lg, _rows(ret_gn_g), l, B, S)
        x2 = _merge(x2, proj, o_mla, o_conv, o_ret, w_mla, w_pw, w_ret, w_o, _rows(ln_mix_post), l)
        x2 = _ffn(x2, _rows(ln_ffn_pre), wg, wu, wd, _rows(ln_ffn_post), l)
    return x2.reshape(B, S, D_MODEL)
```
